```python
import jax, jax.numpy as jnp
from jax import lax
import numpy as np

D_MODEL = 2048
BATCH = 2
SEQ = 8192
DEPTH = 1

CHUNK = 64
M_WIDTH = D_MODEL // 2
R_WIDTH = D_MODEL - M_WIDTH
M_HEADS = 4
M_HEAD_DIM = M_WIDTH // M_HEADS
R_BLOCKS = 8
R_BLOCK_DIM = R_WIDTH // R_BLOCKS
CONV_WIDTH = 4
LRU_C = 8.0
N_GROUPS = 4
EXPERTS_PER_GROUP = 8
N_EXPERTS = N_GROUPS * EXPERTS_PER_GROUP
TOP_K = 2
D_EXPERT = D_MODEL // 4
MOE_BLOCK = 128
EPS = 1e-6
IN_SPLITS = [M_WIDTH, M_WIDTH, M_WIDTH, M_WIDTH, M_HEADS, M_HEADS, R_WIDTH, R_WIDTH]
IN_WIDTH = sum(IN_SPLITS)

kernel_name = 'hybrid_mlstm_rglru_hmoe_block'


def rmsnorm(x):
    x32 = x.astype(jnp.float32)
    return (x32 * lax.rsqrt(jnp.mean(x32 * x32, axis=-1, keepdims=True) + EPS)).astype(x.dtype)


def causal_dwconv(x, w):
    return lax.conv_general_dilated(
        x, w[:, None, :].astype(x.dtype), window_strides=(1,),
        padding=[(CONV_WIDTH - 1, 0)], dimension_numbers=('NWC', 'WIO', 'NWC'),
        feature_group_count=x.shape[-1])


def mlstm_chunkwise(q, k, v, ig, lf):
    B, S, H, Dh = q.shape
    nc = S // CHUNK

    def to_chunks(t):
        t = t.reshape((B, nc, CHUNK, H) + t.shape[3:])
        return jnp.moveaxis(t, (1, 3), (0, 2))

    causal = jnp.tril(jnp.ones((CHUNK, CHUNK), dtype=bool))

    def step(carry, xs):
        C, n, m = carry
        qc, kc, vc, ic, fc = xs
        b = jnp.cumsum(fc, axis=-1)
        b_last = b[..., -1]
        dmat = b[..., :, None] - b[..., None, :] + ic[..., None, :]
        dmat = jnp.where(causal, dmat, -jnp.inf)
        inter = b + m[..., None]
        m_t = jnp.maximum(inter, jnp.max(dmat, axis=-1))
        s = jnp.einsum('bhtd,bhsd->bhts', qc, kc) * jnp.exp(dmat - m_t[..., None])
        e_inter = jnp.exp(inter - m_t)
        num = jnp.einsum('bhts,bhse->bhte', s, vc) + e_inter[..., None] * jnp.einsum('bhtd,bhde->bhte', qc, C)
        den = jnp.sum(s, axis=-1) + e_inter * jnp.einsum('bhtd,bhd->bht', qc, n)
        h = num / jnp.maximum(jnp.abs(den), jnp.exp(-m_t))[..., None]
        g = b_last[..., None] - b + ic
        m_new = jnp.maximum(b_last + m, jnp.max(g, axis=-1))
        wk = jnp.exp(g - m_new[..., None])
        decay = jnp.exp(b_last + m - m_new)
        kw = kc * wk[..., None]
        C_new = decay[..., None, None] * C + jnp.einsum('bhsd,bhse->bhde', kw, vc)
        n_new = decay[..., None] * n + jnp.sum(kw, axis=2)
        return (C_new, n_new, m_new), h

    init = (jnp.zeros((B, H, Dh, Dh), jnp.float32), jnp.zeros((B, H, Dh), jnp.float32),
            jnp.zeros((B, H), jnp.float32))
    _, hs = lax.scan(step, init, (to_chunks(q), to_chunks(k), to_chunks(v), to_chunks(ig), to_chunks(lf)))
    return jnp.moveaxis(hs, (0, 2), (1, 3)).reshape(B, S, H, Dh)


def lru_combine(e1, e2):
    a1, b1 = e1
    a2, b2 = e2
    return a1 * a2, a2 * b1 + b2


def hybrid_mixer(h, w_in, b_gates, conv_qk, mh_norm_g, lru_conv_w, lru_conv_b, w_lru_a, b_lru_a,
                 w_lru_x, b_lru_x, lru_lambda, lru_norm_g, w_out):
    B, S, _ = h.shape
    proj = h @ w_in
    q_raw, k_raw, v_raw, o_raw, i_raw, f_raw, xr_raw, gr_raw = jnp.split(
        proj, list(np.cumsum(IN_SPLITS)[:-1]), axis=-1)

    qk = jax.nn.silu(causal_dwconv(jnp.concatenate([q_raw, k_raw], axis=-1), conv_qk)).astype(jnp.float32)
    q = qk[..., :M_WIDTH].reshape(B, S, M_HEADS, M_HEAD_DIM)
    k = qk[..., M_WIDTH:].reshape(B, S, M_HEADS, M_HEAD_DIM) * (M_HEAD_DIM ** -0.5)
    v = v_raw.astype(jnp.float32).reshape(B, S, M_HEADS, M_HEAD_DIM)
    ig = i_raw.astype(jnp.float32) + b_gates[:M_HEADS]
    lf = jax.nn.log_sigmoid(f_raw.astype(jnp.float32) + b_gates[M_HEADS:])
    hm = mlstm_chunkwise(q, k, v, ig, lf)
    hm = hm * lax.rsqrt(jnp.mean(hm * hm, axis=-1, keepdims=True) + EPS)
    ym = hm.reshape(B, S, M_WIDTH) * mh_norm_g * jax.nn.sigmoid(o_raw.astype(jnp.float32))

    xr = (causal_dwconv(xr_raw, lru_conv_w) + lru_conv_b).astype(jnp.float32)
    xb = xr.reshape(B, S, R_BLOCKS, R_BLOCK_DIM)
    r_gate = jax.nn.sigmoid(jnp.einsum('bsnc,ncd->bsnd', xb, w_lru_a).reshape(B, S, R_WIDTH) + b_lru_a)
    i_gate = jax.nn.sigmoid(jnp.einsum('bsnc,ncd->bsnd', xb, w_lru_x).reshape(B, S, R_WIDTH) + b_lru_x)
    log_a = LRU_C * r_gate * jax.nn.log_sigmoid(lru_lambda.astype(jnp.float32))
    a = jnp.exp(log_a)
    u = jnp.sqrt(-jnp.expm1(2.0 * log_a)) * (i_gate * xr)
    _, hr = lax.associative_scan(lru_combine, (a, u), axis=1)
    yr = hr * jax.nn.gelu(gr_raw.astype(jnp.float32))
    yr = yr.reshape(B, S, R_BLOCKS, R_BLOCK_DIM)
    yr = (yr * lax.rsqrt(jnp.mean(yr * yr, axis=-1, keepdims=True) + EPS)).reshape(B, S, R_WIDTH) * lru_norm_g

    return jnp.concatenate([ym, yr], axis=-1).astype(h.dtype) @ w_out


def hier_moe(xt, w_group, b_group, w_router, b_router, w_e_gate, w_e_up, w_e_down):
    n_tok, d = xt.shape
    xf = xt.astype(jnp.float32)
    p_group = jax.nn.softmax(xf @ w_group + b_group, axis=-1)
    pg_sel, g_sel = lax.top_k(p_group, 1)
    e_logits = (xf @ w_router + b_router).reshape(n_tok, N_GROUPS, EXPERTS_PER_GROUP)
    idx = jnp.broadcast_to(g_sel[:, :, None], (n_tok, 1, EXPERTS_PER_GROUP))
    e_logits = jnp.take_along_axis(e_logits, idx, axis=1)[:, 0]
    pe_sel, e_sel = lax.top_k(jax.nn.softmax(e_logits, axis=-1), TOP_K)
    gate_w = pg_sel * pe_sel / jnp.sum(pe_sel, axis=-1, keepdims=True)
    expert_id = g_sel * EXPERTS_PER_GROUP + e_sel

    m = n_tok * TOP_K
    eid = expert_id.reshape(m)
    tok = jnp.repeat(jnp.arange(n_tok, dtype=jnp.int32), TOP_K)
    wts = gate_w.reshape(m)
    order = jnp.argsort(eid)
    eid_s, tok_s, w_s = eid[order], tok[order], wts[order]
    counts = jnp.bincount(eid, length=N_EXPERTS)
    starts = jnp.cumsum(counts) - counts
    padded = (counts + MOE_BLOCK - 1) // MOE_BLOCK * MOE_BLOCK
    pend = jnp.cumsum(padded)
    pstart = pend - padded
    dest = pstart[eid_s] + jnp.arange(m, dtype=jnp.int32) - starts[eid_s]
    n_slots = -(-m // MOE_BLOCK) * MOE_BLOCK + N_EXPERTS * MOE_BLOCK
    n_blocks = n_slots // MOE_BLOCK
    slot_tok = jnp.zeros((n_slots,), jnp.int32).at[dest].set(tok_s)
    slot_w = jnp.zeros((n_slots,), jnp.float32).at[dest].set(w_s)
    block_e = jnp.minimum(
        jnp.searchsorted(pend, jnp.arange(n_blocks, dtype=pend.dtype) * MOE_BLOCK, side='right'),
        N_EXPERTS - 1)

    def expert_block(args):
        toks, wb, e = args
        xb = xt[toks]
        hb = jax.nn.silu(xb @ w_e_gate[e]) * (xb @ w_e_up[e])
        return (hb @ w_e_down[e]).astype(jnp.float32) * wb[:, None]

    yb = lax.map(expert_block, (slot_tok.reshape(n_blocks, MOE_BLOCK),
                                slot_w.reshape(n_blocks, MOE_BLOCK), block_e))
    return jnp.zeros((n_tok, d), jnp.float32).at[slot_tok].add(yb.reshape(n_slots, d))


def setup_inputs(seed: int = 0) -> dict:
    key = jax.random.key(seed)
    ks = jax.random.split(key, 32)
    L, D, H = DEPTH, D_MODEL, M_HEADS
    nrm = lambda k, shape, s: jax.random.normal(k, shape, jnp.float32) * s
    u = jax.random.uniform(ks[14], (L, R_WIDTH), jnp.float32, minval=0.9, maxval=0.999)
    b_gates = jnp.concatenate([
        nrm(ks[4], (L, H), 0.1),
        jnp.broadcast_to(jnp.linspace(3.0, 6.0, H, dtype=jnp.float32), (L, H)) + nrm(ks[5], (L, H), 0.1)], axis=-1)
    return {
        'x': nrm(ks[0], (BATCH, SEQ, D), 1.0),
        'c': nrm(ks[1], (BATCH, D), 1.0),
        'w_ada': nrm(ks[2], (L, D, 6 * D), D ** -0.5),
        'b_ada': nrm(ks[3], (L, 6 * D), 0.02),
        'w_in': nrm(ks[6], (L, D, IN_WIDTH), D ** -0.5),
        'b_gates': b_gates,
        'conv_qk': nrm(ks[7], (L, CONV_WIDTH, 2 * M_WIDTH), CONV_WIDTH ** -0.5),
        'mh_norm_g': 1.0 + nrm(ks[8], (L, M_WIDTH), 0.02),
        'lru_conv_w': nrm(ks[9], (L, CONV_WIDTH, R_WIDTH), CONV_WIDTH ** -0.5),
        'lru_conv_b': nrm(ks[10], (L, R_WIDTH), 0.02),
        'w_lru_a': nrm(ks[11], (L, R_BLOCKS, R_BLOCK_DIM, R_BLOCK_DIM), R_BLOCK_DIM ** -0.5),
        'b_lru_a': nrm(ks[12], (L, R_WIDTH), 0.02),
        'w_lru_x': nrm(ks[13], (L, R_BLOCKS, R_BLOCK_DIM, R_BLOCK_DIM), R_BLOCK_DIM ** -0.5),
        'b_lru_x': nrm(ks[15], (L, R_WIDTH), 0.02),
        'lru_lambda': jnp.log(u) - jnp.log1p(-u),
        'lru_norm_g': 1.0 + nrm(ks[16], (L, R_WIDTH), 0.02),
        'w_out': nrm(ks[17], (L, D, D), D ** -0.5),
        'w_group': nrm(ks[18], (L, D, N_GROUPS), D ** -0.5),
        'b_group': nrm(ks[19], (L, N_GROUPS), 0.01),
        'w_router': nrm(ks[20], (L, D, N_EXPERTS), D ** -0.5),
        'b_router': nrm(ks[21], (L, N_EXPERTS), 0.01),
        'w_e_gate': nrm(ks[22], (L, N_EXPERTS, D, D_EXPERT), D ** -0.5),
        'w_e_up': nrm(ks[23], (L, N_EXPERTS, D, D_EXPERT), D ** -0.5),
        'w_e_down': nrm(ks[24], (L, N_EXPERTS, D_EXPERT, D), D_EXPERT ** -0.5),
        'final_g': 1.0 + nrm(ks[25], (D,), 0.02),
    }


def reference(x, c, w_ada, b_ada, w_in, b_gates, conv_qk, mh_norm_g, lru_conv_w, lru_conv_b,
              w_lru_a, b_lru_a, w_lru_x, b_lru_x, lru_lambda, lru_norm_g, w_out, w_group, b_group,
              w_router, b_router, w_e_gate, w_e_up, w_e_down, final_g):
    B, S, D = x.shape
    for l in range(DEPTH):
        mod = jax.nn.silu(c) @ w_ada[l] + b_ada[l]
        sh1, sc1, g1, sh2, sc2, g2 = jnp.split(mod, 6, axis=-1)
        hn = rmsnorm(x) * (1.0 + sc1[:, None]) + sh1[:, None]
        mix = hybrid_mixer(hn, w_in[l], b_gates[l], conv_qk[l], mh_norm_g[l], lru_conv_w[l], lru_conv_b[l],
                           w_lru_a[l], b_lru_a[l], w_lru_x[l], b_lru_x[l], lru_lambda[l], lru_norm_g[l], w_out[l])
        x = x + (g1[:, None] * mix).astype(x.dtype)
        hn = rmsnorm(x) * (1.0 + sc2[:, None]) + sh2[:, None]
        y = hier_moe(hn.reshape(B * S, D), w_group[l], b_group[l], w_router[l], b_router[l],
                     w_e_gate[l], w_e_up[l], w_e_down[l]).reshape(B, S, D)
        x = x + (g2[:, None] * y).astype(x.dtype)
    return rmsnorm(x) * final_g
```

```python
import functools

import jax
import jax.numpy as jnp
from jax import lax
from jax.experimental import pallas as pl
from jax.experimental.pallas import tpu as pltpu

F32 = jnp.float32
BF16 = jnp.bfloat16
U32 = jnp.uint32
I32 = jnp.int32

EPS = 1e-6
M_HEADS = 4
R_BLOCKS = 8
CONV_WIDTH = 4
LRU_C = 8.0
N_GROUPS = 4
EXPERTS_PER_GROUP = 8
N_EXPERTS = N_GROUPS * EXPERTS_PER_GROUP
TOP_K = 2

LANES = 128
SUBLANES = 8
VMEM_LIMIT = 56 * 1024 * 1024

MOE_ROWS = 256
HIGHEST = lax.Precision.HIGHEST


def _cparams(sem, vmem=VMEM_LIMIT):
    return pltpu.CompilerParams(dimension_semantics=sem, vmem_limit_bytes=vmem)


def _log_sigmoid(x):
    return jnp.minimum(x, 0.0) - jnp.log1p(jnp.exp(-jnp.abs(x)))


def _silu(x):
    return x * jax.nn.sigmoid(x)


def _causal_conv(cur, prev8, w4):
    rid = lax.broadcasted_iota(I32, (SUBLANES, cur.shape[1]), 0)
    acc = cur * w4[CONV_WIDTH - 1:CONV_WIDTH, :]
    for d in range(1, CONV_WIDTH):
        sh = pltpu.roll(cur, d, axis=0)
        head = jnp.where(rid < d, pltpu.roll(prev8, d, axis=0), sh[0:SUBLANES])
        xd = jnp.concatenate([head, sh[SUBLANES:]], axis=0)
        acc = acc + xd * w4[CONV_WIDTH - 1 - d:CONV_WIDTH - d, :]
    return acc


def _ada_kernel(c_ref, w_ref, b_ref, o_ref):
    s = _silu(c_ref[...])
    o_ref[...] = jnp.dot(s.astype(BF16), w_ref[...].astype(BF16),
                         preferred_element_type=F32) + b_ref[...]


def _ada_mod(c, w_ada, b_ada):
    B, D = c.shape
    n6 = w_ada.shape[1]
    tn = 1024
    cp = jnp.zeros((SUBLANES, D), F32).at[:B].set(c)
    out = pl.pallas_call(
        _ada_kernel,
        grid=(n6 // tn,),
        in_specs=[pl.BlockSpec((SUBLANES, D), lambda j: (0, 0)),
                  pl.BlockSpec((D, tn), lambda j: (0, j)),
                  pl.BlockSpec((1, tn), lambda j: (0, j))],
        out_specs=pl.BlockSpec((SUBLANES, tn), lambda j: (0, j)),
        out_shape=jax.ShapeDtypeStruct((SUBLANES, n6), F32),
        compiler_params=_cparams(("arbitrary",)),
        name="ada_mod",
    )(cp, w_ada, b_ada.reshape(1, n6))
    return out[:B]


def _inproj_kernel(x_ref, mod_ref, w_ref, wif_ref, wift_ref, o_ref, g_ref, gt_ref, hn_ref):
    j = pl.program_id(1)

    @pl.when(j == 0)
    def _():
        x = x_ref[...]
        ms = jnp.mean(x * x, axis=-1, keepdims=True)
        hn = x * lax.rsqrt(ms + EPS) * (1.0 + mod_ref[0, 1:2, :]) + mod_ref[0, 0:1, :]
        hb = hn.astype(BF16)
        hn_ref[...] = hb
        g_ref[...] = jnp.dot(hb, wif_ref[...], preferred_element_type=F32)
        gt_ref[...] = lax.dot_general(wift_ref[...], hb, (((1,), (1,)), ((), ())),
                                      preferred_element_type=F32)

    o_ref[...] = jnp.dot(hn_ref[...], w_ref[...], preferred_element_type=F32)


def _in_proj(x2, mod3, w_main, w_if, w_ift, S, tm, tn):
    N, D = x2.shape
    nw = w_main.shape[1]
    spb = S // tm
    return pl.pallas_call(
        _inproj_kernel,
        grid=(N // tm, nw // tn),
        in_specs=[pl.BlockSpec((tm, D), lambda i, j: (i, 0)),
                  pl.BlockSpec((1, 6, D), lambda i, j: (i // spb, 0, 0)),
                  pl.BlockSpec((D, tn), lambda i, j: (0, j)),
                  pl.BlockSpec((D, LANES), lambda i, j: (0, 0)),
                  pl.BlockSpec((SUBLANES, D), lambda i, j: (0, 0))],
        out_specs=[pl.BlockSpec((tm, tn), lambda i, j: (i, j)),
                   pl.BlockSpec((tm, LANES), lambda i, j: (i, 0)),
                   pl.BlockSpec((SUBLANES, tm), lambda i, j: (0, i))],
        out_shape=[jax.ShapeDtypeStruct((N, nw), F32),
                   jax.ShapeDtypeStruct((N, LANES), F32),
                   jax.ShapeDtypeStruct((SUBLANES, N), F32)],
        scratch_shapes=[pltpu.VMEM((tm, D), BF16)],
        compiler_params=_cparams(("arbitrary", "arbitrary")),
        name="in_proj",
    )(x2, mod3, w_main, w_if, w_ift)


def _mlstm_kernel(qp_ref, kp_ref, q_ref, k_ref, v_ref, o_ref, g_ref, gt_ref, bgr_ref, bgc_ref,
                  cw_ref, ng_ref, y_ref, C_ref, n_ref, m_ref):
    c = pl.program_id(1)
    L = q_ref.shape[1]
    MW = q_ref.shape[2]
    dh = MW // M_HEADS
    H = M_HEADS

    @pl.when(c == 0)
    def _():
        C_ref[...] = jnp.zeros_like(C_ref)
        n_ref[...] = jnp.zeros_like(n_ref)
        m_ref[...] = jnp.zeros_like(m_ref)

    row = lax.broadcasted_iota(I32, (L, L), 0)
    col = lax.broadcasted_iota(I32, (L, L), 1)
    causal = col <= row
    tri = causal.astype(F32)
    tri_t = (row <= col).astype(F32)

    gb = g_ref[0] + bgr_ref[...]
    gtb = gt_ref[...] + bgc_ref[:, 0:1]
    b_col_all = jnp.dot(tri, _log_sigmoid(gb), precision=HIGHEST, preferred_element_type=F32)
    b_row_all = jnp.dot(_log_sigmoid(gtb), tri_t, precision=HIGHEST, preferred_element_type=F32)

    has_prev = c > 0
    qprev = jnp.where(has_prev, qp_ref[0], 0.0)
    kprev = jnp.where(has_prev, kp_ref[0], 0.0)
    q_all = _silu(_causal_conv(q_ref[0], qprev, cw_ref[:, 0:MW]))
    k_all = _silu(_causal_conv(k_ref[0], kprev, cw_ref[:, MW:2 * MW])) * (dh ** -0.5)

    for h in range(H):
        hs = slice(h * dh, (h + 1) * dh)
        ig_col = gb[:, h:h + 1]
        b_col = b_col_all[:, H + h:H + h + 1]
        ig_row = gtb[h:h + 1, :]
        b_row = b_row_all[H + h:H + h + 1, :]
        b_last = b_row[:, L - 1:L]
        m_prev = m_ref[h][0:1, 0:1]

        q = q_all[:, hs]
        k = k_all[:, hs]
        qb = q.astype(BF16)
        kb = k.astype(BF16)
        vb = v_ref[0, :, hs].astype(BF16)

        qk = lax.dot_general(qb, kb, (((1,), (1,)), ((), ())), preferred_element_type=F32)
        dmat = jnp.where(causal, b_col - b_row + ig_row, -jnp.inf)
        inter = b_col + m_prev
        m_t = jnp.maximum(inter, jnp.max(dmat, axis=-1, keepdims=True))
        s = qk * jnp.exp(dmat - m_t)
        e_inter = jnp.exp(inter - m_t)
        C_old = C_ref[h]
        n_old = n_ref[h]
        num = (jnp.dot(s.astype(BF16), vb, preferred_element_type=F32)
               + e_inter * jnp.dot(qb, C_old.astype(BF16), preferred_element_type=F32))
        den = (jnp.sum(s, axis=-1, keepdims=True)
               + e_inter * jnp.sum(q * n_old, axis=-1, keepdims=True))
        hval = num / jnp.maximum(jnp.abs(den), jnp.exp(-m_t))

        g_col = b_last - b_col + ig_col
        g_row = b_last - b_row + ig_row
        m_new = jnp.maximum(b_last + m_prev, jnp.max(g_row, axis=-1, keepdims=True))
        wk = jnp.exp(g_col - m_new)
        decay = jnp.exp(b_last + m_prev - m_new)
        kw = k * wk
        C_ref[h] = decay * C_old + lax.dot_general(kw.astype(BF16), vb, (((0,), (0,)), ((), ())),
                                                   preferred_element_type=F32)
        n_ref[h] = decay * n_old + jnp.sum(kw, axis=0, keepdims=True)
        m_ref[h] = jnp.broadcast_to(m_new, m_ref.shape[1:])

        hnorm = hval * lax.rsqrt(jnp.mean(hval * hval, axis=-1, keepdims=True) + EPS)
        ym = hnorm * ng_ref[:, hs] * jax.nn.sigmoid(o_ref[0, :, hs])
        y_ref[0, :, hs] = ym.astype(BF16)


def _mlstm(proj3, gates3, gates_t, bg_row, bg_col, conv_qk, mh_norm_g, L):
    B, S, _ = proj3.shape
    MW = mh_norm_g.shape[-1]
    dh = MW // M_HEADS
    nc = S // L
    l8 = L // SUBLANES

    def prev_map(colblk):
        return lambda b, c: (b, jnp.maximum(c * l8 - 1, 0), colblk)

    def cur_map(colblk):
        return lambda b, c: (b, c, colblk)

    return pl.pallas_call(
        _mlstm_kernel,
        grid=(B, nc),
        in_specs=[pl.BlockSpec((1, SUBLANES, MW), prev_map(0)),
                  pl.BlockSpec((1, SUBLANES, MW), prev_map(1)),
                  pl.BlockSpec((1, L, MW), cur_map(0)),
                  pl.BlockSpec((1, L, MW), cur_map(1)),
                  pl.BlockSpec((1, L, MW), cur_map(2)),
                  pl.BlockSpec((1, L, MW), cur_map(3)),
                  pl.BlockSpec((1, L, LANES), lambda b, c: (b, c, 0)),
                  pl.BlockSpec((SUBLANES, L), lambda b, c: (0, b * nc + c)),
                  pl.BlockSpec((1, LANES), lambda b, c: (0, 0)),
                  pl.BlockSpec((SUBLANES, LANES), lambda b, c: (0, 0)),
                  pl.BlockSpec((CONV_WIDTH, 2 * MW), lambda b, c: (0, 0)),
                  pl.BlockSpec((1, MW), lambda b, c: (0, 0))],
        out_specs=pl.BlockSpec((1, L, MW), lambda b, c: (b, c, 0)),
        out_shape=jax.ShapeDtypeStruct((B, S, MW), BF16),
        scratch_shapes=[pltpu.VMEM((M_HEADS, dh, dh), F32),
                        pltpu.VMEM((M_HEADS, 1, dh), F32),
                        pltpu.VMEM((M_HEADS, SUBLANES, LANES), F32)],
        compiler_params=_cparams(("arbitrary", "arbitrary")),
        name="mlstm",
    )(proj3, proj3, proj3, proj3, proj3, proj3, gates3, gates_t, bg_row, bg_col, conv_qk,
      mh_norm_g)


def _lru_scan(a, u, h0):
    T = a.shape[0]
    rid = lax.broadcasted_iota(I32, a.shape, 0)
    k = 1
    while k < T:
        keep = rid >= k
        a_sh = jnp.where(keep, pltpu.roll(a, k, axis=0), 1.0)
        u_sh = jnp.where(keep, pltpu.roll(u, k, axis=0), 0.0)
        u = a * u_sh + u
        a = a * a_sh
        k *= 2
    return u + a * h0


def _rglru_kernel(xp_ref, x_ref, gr_ref, cw_ref, cb_ref, wax_ref, ba_ref, bx_ref, lam_ref, ng_ref,
                  y_ref, h_ref):
    t = pl.program_id(1)
    T = x_ref.shape[1]
    RW = x_ref.shape[2]
    bd = RW // R_BLOCKS

    @pl.when(t == 0)
    def _():
        h_ref[...] = jnp.zeros_like(h_ref)

    prev = jnp.where(t > 0, xp_ref[0], 0.0)
    xr = _causal_conv(x_ref[0], prev, cw_ref[...]) + cb_ref[...]
    xrb = xr.astype(BF16)
    ls = _log_sigmoid(lam_ref[...])
    for n in range(R_BLOCKS):
        sl = slice(n * bd, (n + 1) * bd)
        z = jnp.dot(xrb[:, sl], wax_ref[n], preferred_element_type=F32)
        r_gate = jax.nn.sigmoid(z[:, :bd] + ba_ref[:, sl])
        i_gate = jax.nn.sigmoid(z[:, bd:] + bx_ref[:, sl])
        log_a = LRU_C * r_gate * ls[:, sl]
        a = jnp.exp(log_a)
        u = jnp.sqrt(-jnp.tanh(log_a) * (a * a + 1.0)) * (i_gate * xr[:, sl])
        hseq = _lru_scan(a, u, h_ref[:, sl])
        h_ref[:, sl] = hseq[T - 1:T, :]
        y = hseq * jax.nn.gelu(gr_ref[0, :, sl])
        y = y * lax.rsqrt(jnp.mean(y * y, axis=-1, keepdims=True) + EPS) * ng_ref[:, sl]
        y_ref[0, :, sl] = y.astype(BF16)


def _rglru(proj3, lru_conv_w, lru_conv_b, wax, b_a, b_x, lam, ng, T, xr_blk, gr_blk):
    B, S, _ = proj3.shape
    RW = lam.shape[-1]
    bd = RW // R_BLOCKS
    t8 = T // SUBLANES
    vec = pl.BlockSpec((1, RW), lambda b, t: (0, 0))
    return pl.pallas_call(
        _rglru_kernel,
        grid=(B, S // T),
        in_specs=[pl.BlockSpec((1, SUBLANES, RW), lambda b, t: (b, jnp.maximum(t * t8 - 1, 0), xr_blk)),
                  pl.BlockSpec((1, T, RW), lambda b, t: (b, t, xr_blk)),
                  pl.BlockSpec((1, T, RW), lambda b, t: (b, t, gr_blk)),
                  pl.BlockSpec((CONV_WIDTH, RW), lambda b, t: (0, 0)),
                  vec,
                  pl.BlockSpec((R_BLOCKS, bd, 2 * bd), lambda b, t: (0, 0, 0)),
                  vec, vec, vec, vec],
        out_specs=pl.BlockSpec((1, T, RW), lambda b, t: (b, t, 0)),
        out_shape=jax.ShapeDtypeStruct((B, S, RW), BF16),
        scratch_shapes=[pltpu.VMEM((1, RW), F32)],
        compiler_params=_cparams(("arbitrary", "arbitrary")),
        name="rglru",
    )(proj3, proj3, proj3, lru_conv_w, lru_conv_b, wax, b_a, b_x, lam, ng)


def _outproj_kernel(x_ref, ym_ref, yr_ref, mod_ref, wm_ref, wr_ref, wrt_ref, brt_ref,
                    x1_ref, hp_ref, lt_ref):
    D = x_ref.shape[1]
    mix = (jnp.dot(ym_ref[...], wm_ref[...], preferred_element_type=F32)
           + jnp.dot(yr_ref[...], wr_ref[...], preferred_element_type=F32))
    x1 = x_ref[...] + mod_ref[0, 2:3, :] * mix
    x1_ref[...] = x1
    ms = jnp.mean(x1 * x1, axis=-1, keepdims=True)
    hn = x1 * lax.rsqrt(ms + EPS) * (1.0 + mod_ref[0, 4:5, :]) + mod_ref[0, 3:4, :]
    hb = hn.astype(BF16)
    lt_ref[...] = lax.dot_general(wrt_ref[...], hb, (((1,), (1,)), ((), ())),
                                  preferred_element_type=F32) + brt_ref[:, 0:1]
    hi = lax.bitcast_convert_type(hb[:, :D // 2].astype(F32), U32)
    lo = lax.bitcast_convert_type(hb[:, D // 2:].astype(F32), U32)
    hp_ref[...] = hi | (lo >> 16)


def _out_proj(x2, ym2, yr2, mod3, w_om, w_or, w_rt_t, b_rt, S, tm):
    N, D = x2.shape
    MW = ym2.shape[1]
    RW = yr2.shape[1]
    NR = w_rt_t.shape[0]
    spb = S // tm
    return pl.pallas_call(
        _outproj_kernel,
        grid=(N // tm,),
        in_specs=[pl.BlockSpec((tm, D), lambda i: (i, 0)),
                  pl.BlockSpec((tm, MW), lambda i: (i, 0)),
                  pl.BlockSpec((tm, RW), lambda i: (i, 0)),
                  pl.BlockSpec((1, 6, D), lambda i: (i // spb, 0, 0)),
                  pl.BlockSpec((MW, D), lambda i: (0, 0)),
                  pl.BlockSpec((RW, D), lambda i: (0, 0)),
                  pl.BlockSpec((NR, D), lambda i: (0, 0)),
                  pl.BlockSpec((NR, LANES), lambda i: (0, 0))],
        out_specs=[pl.BlockSpec((tm, D), lambda i: (i, 0)),
                   pl.BlockSpec((tm, D // 2), lambda i: (i, 0)),
                   pl.BlockSpec((NR, tm), lambda i: (0, i))],
        out_shape=[jax.ShapeDtypeStruct((N, D), F32),
                   jax.ShapeDtypeStruct((N, D // 2), U32),
                   jax.ShapeDtypeStruct((NR, N), F32)],
        compiler_params=_cparams(("arbitrary",)),
        name="out_proj",
    )(x2, ym2, yr2, mod3, w_om, w_or, w_rt_t, b_rt)


def _router_kernel(lt_ref, r_ref, cnt_ref, carry_ref):
    i = pl.program_id(0)
    tm = lt_ref.shape[1]
    E8 = EXPERTS_PER_GROUP

    @pl.when(i == 0)
    def _():
        carry_ref[...] = jnp.zeros_like(carry_ref)

    sub = lax.broadcasted_iota(I32, (SUBLANES, tm), 0)
    gl = jnp.where(sub < N_GROUPS, lt_ref[N_EXPERTS:N_EXPERTS + SUBLANES, :], -jnp.inf)
    ge = jnp.exp(gl - jnp.max(gl, axis=0, keepdims=True))
    pg = ge / jnp.sum(ge, axis=0, keepdims=True)
    pg_sel = jnp.max(pg, axis=0, keepdims=True)
    g_sel = jnp.min(jnp.where(pg == pg_sel, sub, SUBLANES), axis=0, keepdims=True)

    el = lt_ref[(N_GROUPS - 1) * E8:N_GROUPS * E8, :]
    for g in range(N_GROUPS - 2, -1, -1):
        el = jnp.where(g_sel == g, lt_ref[g * E8:(g + 1) * E8, :], el)
    ee = jnp.exp(el - jnp.max(el, axis=0, keepdims=True))
    pe = ee / jnp.sum(ee, axis=0, keepdims=True)
    p0 = jnp.max(pe, axis=0, keepdims=True)
    i0 = jnp.min(jnp.where(pe == p0, sub, SUBLANES), axis=0, keepdims=True)
    pe1 = jnp.where(sub == i0, -1.0, pe)
    p1 = jnp.max(pe1, axis=0, keepdims=True)
    i1 = jnp.min(jnp.where(pe1 == p1, sub, SUBLANES), axis=0, keepdims=True)
    psum = p0 + p1
    w0 = pg_sel * p0 / psum
    w1 = pg_sel * p1 / psum
    e0 = g_sel * E8 + i0
    e1 = g_sel * E8 + i1

    eid = lax.broadcasted_iota(I32, (N_EXPERTS, tm), 0)
    oh0 = (eid == e0).astype(F32)
    oh1 = (eid == e1).astype(F32)
    oh = oh0 + oh1
    r_ = lax.broadcasted_iota(I32, (tm, tm), 0)
    c_ = lax.broadcasted_iota(I32, (tm, tm), 1)
    before = (r_ < c_).astype(BF16)
    cnt = carry_ref[:, 0:1] + jnp.dot(oh.astype(BF16), before, preferred_element_type=F32)
    pos0 = jnp.sum(oh0 * cnt, axis=0, keepdims=True)
    pos1 = jnp.sum(oh1 * cnt, axis=0, keepdims=True)
    new_carry = carry_ref[...] + jnp.sum(oh, axis=1, keepdims=True)
    carry_ref[...] = new_carry
    cnt_ref[...] = new_carry

    out = jnp.where(sub == 0, e0.astype(F32), 0.0)
    out = jnp.where(sub == 1, e1.astype(F32), out)
    out = jnp.where(sub == 2, pos0, out)
    out = jnp.where(sub == 3, pos1, out)
    out = jnp.where(sub == 4, w0, out)
    out = jnp.where(sub == 5, w1, out)
    r_ref[...] = out


def _router(logits_t, tm):
    NR, N = logits_t.shape
    return pl.pallas_call(
        _router_kernel,
        grid=(N // tm,),
        in_specs=[pl.BlockSpec((NR, tm), lambda i: (0, i))],
        out_specs=[pl.BlockSpec((SUBLANES, tm), lambda i: (0, i)),
                   pl.BlockSpec((N_EXPERTS, LANES), lambda i: (0, 0))],
        out_shape=[jax.ShapeDtypeStruct((SUBLANES, N), F32),
                   jax.ShapeDtypeStruct((N_EXPERTS, LANES), F32)],
        scratch_shapes=[pltpu.VMEM((N_EXPERTS, LANES), F32)],
        compiler_params=_cparams(("arbitrary",)),
        name="router",
    )(logits_t)


def _row_copy(src, dst, sem):
    return pltpu.make_async_copy(src, dst, sem)


def _dispatch_kernel(dest_ref, hp_ref, xs_in_ref, xs_ref, sem):
    del xs_in_ref
    tm = hp_ref.shape[0]

    def issue(t, carry):
        for kk in range(TOP_K):
            d = dest_ref[kk, t]
            _row_copy(hp_ref.at[pl.ds(t, 1)], xs_ref.at[pl.ds(d, 1)], sem.at[0]).start()
        return carry

    lax.fori_loop(0, tm, issue, 0)

    def drain(t, carry):
        _row_copy(hp_ref.at[pl.ds(0, 1)], xs_ref.at[pl.ds(0, 1)], sem.at[0]).wait()
        return carry

    lax.fori_loop(0, TOP_K * tm, drain, 0)


def _dispatch(dest, hp, n_slots, tm):
    N, W = hp.shape
    xs0 = jnp.zeros((n_slots, W), U32)
    return pl.pallas_call(
        _dispatch_kernel,
        grid=(N // tm,),
        in_specs=[pl.BlockSpec((TOP_K, tm), lambda i: (0, i), memory_space=pltpu.SMEM),
                  pl.BlockSpec((tm, W), lambda i: (i, 0)),
                  pl.BlockSpec(memory_space=pl.ANY)],
        out_specs=pl.BlockSpec(memory_space=pl.ANY),
        out_shape=jax.ShapeDtypeStruct((n_slots, W), U32),
        scratch_shapes=[pltpu.SemaphoreType.DMA((1,))],
        input_output_aliases={2: 0},
        compiler_params=_cparams(("arbitrary",)),
        name="dispatch",
    )(dest, hp, xs0)


def _expert_kernel(be_ref, nu_ref, xs_ref, wg_ref, wu_ref, wd_ref, ys_ref, wgu_s, wd_s):
    i = pl.program_id(0)
    DE = wg_ref.shape[2]
    half = xs_ref.shape[1]
    e = be_ref[i]
    e_prev = be_ref[jnp.maximum(i - 1, 0)]
    used = i < nu_ref[0]

    @pl.when(jnp.logical_and(used, jnp.logical_or(i == 0, e != e_prev)))
    def _():
        wgu_s[:, :DE] = wg_ref[0].astype(BF16)
        wgu_s[:, DE:] = wu_ref[0].astype(BF16)
        wd_s[...] = wd_ref[0].astype(BF16)

    @pl.when(used)
    def _():
        w = xs_ref[...]
        x_hi = lax.bitcast_convert_type(w & jnp.uint32(0xFFFF0000), F32).astype(BF16)
        x_lo = lax.bitcast_convert_type(w << 16, F32).astype(BF16)
        gu = (jnp.dot(x_hi, wgu_s[:half, :], preferred_element_type=F32)
              + jnp.dot(x_lo, wgu_s[half:, :], preferred_element_type=F32))
        hb = _silu(gu[:, :DE]) * gu[:, DE:]
        ys_ref[...] = jnp.dot(hb.astype(BF16), wd_s[...], preferred_element_type=F32)

    @pl.when(jnp.logical_not(used))
    def _():
        ys_ref[...] = jnp.zeros_like(ys_ref)


def _experts(block_e, n_used, xs, w_e_gate, w_e_up, w_e_down):
    n_slots, half = xs.shape
    _, D, DE = w_e_gate.shape
    n_blocks = n_slots // MOE_ROWS
    grid_spec = pltpu.PrefetchScalarGridSpec(
        num_scalar_prefetch=2,
        grid=(n_blocks,),
        in_specs=[pl.BlockSpec((MOE_ROWS, half), lambda i, be, nu: (i, 0)),
                  pl.BlockSpec((1, D, DE), lambda i, be, nu: (be[i], 0, 0)),
                  pl.BlockSpec((1, D, DE), lambda i, be, nu: (be[i], 0, 0)),
                  pl.BlockSpec((1, DE, D), lambda i, be, nu: (be[i], 0, 0))],
        out_specs=pl.BlockSpec((MOE_ROWS, D), lambda i, be, nu: (i, 0)),
        scratch_shapes=[pltpu.VMEM((D, 2 * DE), BF16), pltpu.VMEM((DE, D), BF16)],
    )
    return pl.pallas_call(
        _expert_kernel,
        grid_spec=grid_spec,
        out_shape=jax.ShapeDtypeStruct((n_slots, D), F32),
        compiler_params=_cparams(("arbitrary",)),
        name="experts",
    )(block_e, n_used, xs, w_e_gate, w_e_up, w_e_down)


def _combine_kernel(dcur_ref, dnext_ref, x1_ref, wt_ref, mod_ref, fg_ref, ys_ref, o_ref, gbuf, sem):
    i = pl.program_id(0)
    n = pl.num_programs(0)
    tm = x1_ref.shape[0]
    slot = i % 2

    def gather(dref, s):
        def issue(t, carry):
            for kk in range(TOP_K):
                d = dref[kk, t]
                _row_copy(ys_ref.at[pl.ds(d, 1)], gbuf.at[s, kk, pl.ds(t, 1)], sem.at[s]).start()
            return carry
        lax.fori_loop(0, tm, issue, 0)

    @pl.when(i == 0)
    def _():
        gather(dcur_ref, 0)

    @pl.when(i + 1 < n)
    def _():
        gather(dnext_ref, 1 - slot)

    def drain(t, carry):
        _row_copy(ys_ref.at[pl.ds(0, 1)], gbuf.at[slot, 0, pl.ds(0, 1)], sem.at[slot]).wait()
        return carry

    lax.fori_loop(0, TOP_K * tm, drain, 0)

    wt = wt_ref[...]
    y = wt[:, 0:1] * gbuf[slot, 0] + wt[:, 1:2] * gbuf[slot, 1]
    x2 = x1_ref[...] + mod_ref[0, 5:6, :] * y
    ms = jnp.mean(x2 * x2, axis=-1, keepdims=True)
    o_ref[...] = x2 * lax.rsqrt(ms + EPS) * fg_ref[...]


def _combine(dest, x1, wts, mod3, final_g, ys, S, tm):
    N, D = x1.shape
    spb = S // tm
    nsteps = N // tm
    return pl.pallas_call(
        _combine_kernel,
        grid=(nsteps,),
        in_specs=[pl.BlockSpec((TOP_K, tm), lambda i: (0, i), memory_space=pltpu.SMEM),
                  pl.BlockSpec((TOP_K, tm), lambda i: (0, jnp.minimum(i + 1, nsteps - 1)),
                               memory_space=pltpu.SMEM),
                  pl.BlockSpec((tm, D), lambda i: (i, 0)),
                  pl.BlockSpec((tm, SUBLANES), lambda i: (i, 0)),
                  pl.BlockSpec((1, 6, D), lambda i: (i // spb, 0, 0)),
                  pl.BlockSpec((1, D), lambda i: (0, 0)),
                  pl.BlockSpec(memory_space=pl.ANY)],
        out_specs=pl.BlockSpec((tm, D), lambda i: (i, 0)),
        out_shape=jax.ShapeDtypeStruct((N, D), F32),
        scratch_shapes=[pltpu.VMEM((2, TOP_K, tm, D), F32), pltpu.SemaphoreType.DMA((2,))],
        compiler_params=_cparams(("arbitrary",)),
        name="combine",
    )(dest, dest, x1, wts, mod3, final_g.reshape(1, D), ys)


def _tiles(S):
    def fit(t):
        return min(t, S)
    return dict(inproj_tm=fit(1024), inproj_tn=1024, chunk=fit(256), lru_t=fit(256),
                outproj_tm=fit(256), router_tm=fit(512), dispatch_tm=fit(256), combine_tm=fit(256))


def _layer(x, mod, w_in, b_gates, conv_qk, mh_norm_g, lru_conv_w, lru_conv_b, w_lru_a, b_lru_a,
           w_lru_x, b_lru_x, lru_lambda, lru_norm_g, w_out, w_group, b_group, w_router, b_router,
           w_e_gate, w_e_up, w_e_down, final_g):
    B, S, D = x.shape
    N = B * S
    MW = mh_norm_g.shape[-1]
    RW = lru_lambda.shape[-1]
    H = M_HEADS
    tl = _tiles(S)
    x2 = x.reshape(N, D)
    mod3 = mod.reshape(B, 6, D)

    n_qkvo = 4 * MW
    w_main = jnp.concatenate([w_in[:, :n_qkvo], w_in[:, n_qkvo + 2 * H:]], axis=1).astype(BF16)
    w_gate_cols = w_in[:, n_qkvo:n_qkvo + 2 * H]
    w_if = jnp.zeros((D, LANES), F32).at[:, :2 * H].set(w_gate_cols).astype(BF16)
    w_ift = jnp.zeros((SUBLANES, D), F32).at[:2 * H].set(w_gate_cols.T).astype(BF16)
    proj, gates, gates_t = _in_proj(x2, mod3, w_main, w_if, w_ift, S, tl["inproj_tm"],
                                    tl["inproj_tn"])
    proj3 = proj.reshape(B, S, -1)

    bg_row = jnp.zeros((1, LANES), F32).at[0, :2 * H].set(b_gates)
    bg_col = jnp.zeros((SUBLANES, LANES), F32).at[:2 * H, :].set(b_gates[:, None])
    ym = _mlstm(proj3, gates.reshape(B, S, LANES), gates_t, bg_row, bg_col, conv_qk,
                mh_norm_g.reshape(1, MW), tl["chunk"])

    wax = jnp.concatenate([w_lru_a, w_lru_x], axis=-1).astype(BF16)
    yr = _rglru(proj3, lru_conv_w, lru_conv_b.reshape(1, RW), wax, b_lru_a.reshape(1, RW),
                b_lru_x.reshape(1, RW), lru_lambda.reshape(1, RW), lru_norm_g.reshape(1, RW),
                tl["lru_t"], n_qkvo // RW, n_qkvo // RW + 1)

    NR = N_EXPERTS + SUBLANES
    w_rt_t = (jnp.zeros((NR, D), F32).at[:N_EXPERTS].set(w_router.T)
              .at[N_EXPERTS:N_EXPERTS + N_GROUPS].set(w_group.T)).astype(BF16)
    b_rt = (jnp.zeros((NR, LANES), F32).at[:N_EXPERTS, :].set(b_router[:, None])
            .at[N_EXPERTS:N_EXPERTS + N_GROUPS, :].set(b_group[:, None]))
    w_out_b = w_out.astype(BF16)
    x1, hp, logits_t = _out_proj(x2, ym.reshape(N, MW), yr.reshape(N, RW), mod3, w_out_b[:MW],
                                 w_out_b[MW:], w_rt_t, b_rt, S, tl["outproj_tm"])

    route, counts = _router(logits_t, tl["router_tm"])

    cnt = counts[:, 0].astype(I32)
    padded = (cnt + MOE_ROWS - 1) // MOE_ROWS * MOE_ROWS
    pend = jnp.cumsum(padded)
    pstart = pend - padded
    e_sel = route[0:TOP_K].astype(I32)
    dest = pstart[e_sel] + route[TOP_K:2 * TOP_K].astype(I32)
    n_slots = N * TOP_K + N_EXPERTS * MOE_ROWS
    n_blocks = n_slots // MOE_ROWS
    block_e = jnp.minimum(
        jnp.searchsorted(pend, jnp.arange(n_blocks, dtype=I32) * MOE_ROWS, side='right'),
        N_EXPERTS - 1).astype(I32)
    n_used = (pend[-1:] // MOE_ROWS).astype(I32)
    wts = route[2 * TOP_K:].T

    xs = _dispatch(dest, hp, n_slots, tl["dispatch_tm"])
    ys = _experts(block_e, n_used, xs, w_e_gate, w_e_up, w_e_down)
    wts8 = jnp.zeros((N, SUBLANES), F32).at[:, :wts.shape[1]].set(wts)
    out = _combine(dest, x1, wts8, mod3, final_g, ys, S, tl["combine_tm"])
    return out.reshape(B, S, D)


def kernel(x, c, w_ada, b_ada, w_in, b_gates, conv_qk, mh_norm_g, lru_conv_w, lru_conv_b, w_lru_a, b_lru_a, w_lru_x, b_lru_x, lru_lambda, lru_norm_g, w_out, w_group, b_group, w_router, b_router, w_e_gate, w_e_up, w_e_down, final_g):
    depth = w_ada.shape[0]
    assert depth == 1, "single trunk layer"
    l = 0
    mod = _ada_mod(c, w_ada[l], b_ada[l])
    return _layer(x, mod, w_in[l], b_gates[l], conv_qk[l], mh_norm_g[l], lru_conv_w[l],
                  lru_conv_b[l], w_lru_a[l], b_lru_a[l], w_lru_x[l], b_lru_x[l], lru_lambda[l],
                  lru_norm_g[l], w_out[l], w_group[l], b_group[l], w_router[l], b_router[l],
                  w_e_gate[l], w_e_up[l], w_e_down[l], final_g)
```

```python
import functools

import jax
import jax.numpy as jnp
from jax import lax
from jax.experimental import pallas as pl
from jax.experimental.pallas import tpu as pltpu

F32 = jnp.float32
BF16 = jnp.bfloat16
U32 = jnp.uint32
I32 = jnp.int32

EPS = 1e-6
M_HEADS = 4
R_BLOCKS = 8
CONV_WIDTH = 4
LRU_C = 8.0
N_GROUPS = 4
EXPERTS_PER_GROUP = 8
N_EXPERTS = N_GROUPS * EXPERTS_PER_GROUP
TOP_K = 2

LANES = 128
SUBLANES = 8
VMEM_LIMIT = 56 * 1024 * 1024

MOE_ROWS = 256
META_BLOCK_E, META_NEXT_E, META_SLOT, META_NUSED, META_PAD_LO, META_PAD_N = range(6)


def _cparams(sem, vmem=VMEM_LIMIT):
    return pltpu.CompilerParams(dimension_semantics=sem, vmem_limit_bytes=vmem)


def _log_sigmoid(x):
    return jnp.minimum(x, 0.0) - jnp.log1p(jnp.exp(-jnp.abs(x)))


def _silu(x):
    return x * jax.nn.sigmoid(x)


def _split3(x):
    hi = x.astype(BF16)
    r1 = x - hi.astype(F32)
    mid = r1.astype(BF16)
    lo = (r1 - mid.astype(F32)).astype(BF16)
    return hi, mid, lo


def _causal_conv(cur, prev8, w4):
    T, C = cur.shape
    G = T // SUBLANES
    x3 = cur.reshape(G, SUBLANES, C)
    p3 = prev8.reshape(1, SUBLANES, C)
    rid = lax.broadcasted_iota(I32, x3.shape, 1)
    acc = x3 * w4[CONV_WIDTH - 1:CONV_WIDTH, :]
    for d in range(1, CONV_WIDTH):
        rot = pltpu.roll(x3, d, axis=1)
        before = jnp.concatenate([pltpu.roll(p3, d, axis=1), rot[:G - 1]], axis=0)
        acc = acc + jnp.where(rid < d, before, rot) * w4[CONV_WIDTH - 1 - d:CONV_WIDTH - d, :]
    return acc.reshape(T, C)


def _ada_kernel(c_ref, w_ref, b_ref, o_ref):
    s = _silu(c_ref[...])
    o_ref[...] = jnp.dot(s.astype(BF16), w_ref[...].astype(BF16),
                         preferred_element_type=F32) + b_ref[...]


def _ada_mod(c, w_ada, b_ada):
    B, D = c.shape
    n6 = w_ada.shape[1]
    tn = 1024
    cp = jnp.zeros((SUBLANES, D), F32).at[:B].set(c)
    out = pl.pallas_call(
        _ada_kernel,
        grid=(n6 // tn,),
        in_specs=[pl.BlockSpec((SUBLANES, D), lambda j: (0, 0)),
                  pl.BlockSpec((D, tn), lambda j: (0, j)),
                  pl.BlockSpec((1, tn), lambda j: (0, j))],
        out_specs=pl.BlockSpec((SUBLANES, tn), lambda j: (0, j)),
        out_shape=jax.ShapeDtypeStruct((SUBLANES, n6), F32),
        compiler_params=_cparams(("arbitrary",)),
        name="ada_mod",
    )(cp, w_ada, b_ada.reshape(1, n6))
    return out[:B]


_NT = (((1,), (1,)), ((), ()))


def _prep_kernel(w_ref, wm_ref):
    wm_ref[...] = w_ref[...].astype(BF16)


def _prep_w_in(w_in_t, n_qkvo, tr):
    nin, D = w_in_t.shape
    n_gate = 2 * M_HEADS
    nm = nin - n_gate
    assert n_qkvo % tr == 0 and nm % tr == 0 and n_gate % SUBLANES == 0
    return pl.pallas_call(
        _prep_kernel,
        grid=(nm // tr,),
        in_specs=[pl.BlockSpec((pl.Element(tr), pl.Element(D)),
                               lambda i: (pl.multiple_of(
                                   i * tr + jnp.where(i * tr < n_qkvo, 0, n_gate), SUBLANES), 0))],
        out_specs=pl.BlockSpec((tr, D), lambda i: (i, 0)),
        out_shape=jax.ShapeDtypeStruct((nm, D), BF16),
        compiler_params=_cparams(("arbitrary",)),
        name="prep_w_in",
    )(w_in_t)


def _inproj_kernel(x_ref, mod_ref, w_ref, wif_ref, wift_ref, o_ref, g_ref, gt_ref, hn_ref):
    j = pl.program_id(1)

    @pl.when(j == 0)
    def _():
        x = x_ref[...]
        ms = jnp.mean(x * x, axis=-1, keepdims=True)
        hn = x * lax.rsqrt(ms + EPS) * (1.0 + mod_ref[0, 1:2, :]) + mod_ref[0, 0:1, :]
        hb = hn.astype(BF16)
        hn_ref[...] = hb
        g_ref[...] = lax.dot_general(hb, wif_ref[...], _NT, preferred_element_type=F32)
        gt_ref[...] = lax.dot_general(wift_ref[...], hb, _NT, preferred_element_type=F32)

    o_ref[...] = lax.dot_general(hn_ref[...], w_ref[...], _NT, preferred_element_type=F32)


def _in_proj(x2, mod3, w_main_t, w_if_t, w_ift, S, tm, tn):
    N, D = x2.shape
    nw = w_main_t.shape[0]
    spb = S // tm
    return pl.pallas_call(
        _inproj_kernel,
        grid=(N // tm, nw // tn),
        in_specs=[pl.BlockSpec((tm, D), lambda i, j: (i, 0)),
                  pl.BlockSpec((1, 6, D), lambda i, j: (i // spb, 0, 0)),
                  pl.BlockSpec((tn, D), lambda i, j: (j, 0)),
                  pl.BlockSpec((LANES, D), lambda i, j: (0, 0)),
                  pl.BlockSpec((SUBLANES, D), lambda i, j: (0, 0))],
        out_specs=[pl.BlockSpec((tm, tn), lambda i, j: (i, j)),
                   pl.BlockSpec((tm, LANES), lambda i, j: (i, 0)),
                   pl.BlockSpec((SUBLANES, tm), lambda i, j: (0, i))],
        out_shape=[jax.ShapeDtypeStruct((N, nw), F32),
                   jax.ShapeDtypeStruct((N, LANES), F32),
                   jax.ShapeDtypeStruct((SUBLANES, N), F32)],
        scratch_shapes=[pltpu.VMEM((tm, D), BF16)],
        compiler_params=_cparams(("arbitrary", "arbitrary")),
        name="in_proj",
    )(x2, mod3, w_main_t, w_if_t, w_ift)


def _mlstm_kernel(qp_ref, kp_ref, q_ref, k_ref, v_ref, o_ref, g_ref, gt_ref, bgr_ref, bgc_ref,
                  cw_ref, ng_ref, y_ref, C_ref, n_ref, m_ref):
    c = pl.program_id(1)
    L = q_ref.shape[1]
    MW = q_ref.shape[2]
    dh = MW // M_HEADS
    H = M_HEADS

    @pl.when(c == 0)
    def _():
        C_ref[...] = jnp.zeros_like(C_ref)
        n_ref[...] = jnp.zeros_like(n_ref)
        m_ref[...] = jnp.zeros_like(m_ref)

    row = lax.broadcasted_iota(I32, (L, L), 0)
    col = lax.broadcasted_iota(I32, (L, L), 1)
    causal = col <= row
    tri = causal.astype(BF16)
    tri_t = (row <= col).astype(BF16)

    gb = g_ref[0] + bgr_ref[...]
    gtb = gt_ref[...] + bgc_ref[:, 0:1]
    b_col_all = sum(jnp.dot(tri, p, preferred_element_type=F32) for p in _split3(_log_sigmoid(gb)))
    b_row_all = sum(jnp.dot(p, tri_t, preferred_element_type=F32) for p in _split3(_log_sigmoid(gtb)))

    has_prev = c > 0

    for h in range(H):
        hs = slice(h * dh, (h + 1) * dh)
        ig_col = gb[:, h:h + 1]
        b_col = b_col_all[:, H + h:H + h + 1]
        ig_row = gtb[h:h + 1, :]
        b_row = b_row_all[H + h:H + h + 1, :]
        b_last = b_row[:, L - 1:L]
        m_prev = m_ref[h][0:1, 0:1]

        qprev = jnp.where(has_prev, qp_ref[0, :, hs], 0.0)
        kprev = jnp.where(has_prev, kp_ref[0, :, hs], 0.0)
        q = _silu(_causal_conv(q_ref[0, :, hs], qprev, cw_ref[:, hs]))
        k = _silu(_causal_conv(k_ref[0, :, hs], kprev,
                               cw_ref[:, MW + h * dh:MW + (h + 1) * dh])) * (dh ** -0.5)
        qb = q.astype(BF16)
        kb = k.astype(BF16)
        vb = v_ref[0, :, hs].astype(BF16)

        qk = lax.dot_general(qb, kb, (((1,), (1,)), ((), ())), preferred_element_type=F32)
        dmat = jnp.where(causal, b_col - b_row + ig_row, -jnp.inf)
        inter = b_col + m_prev
        m_t = jnp.maximum(inter, jnp.max(dmat, axis=-1, keepdims=True))
        s = qk * jnp.exp(dmat - m_t)
        e_inter = jnp.exp(inter - m_t)
        C_old = C_ref[h]
        n_old = n_ref[h]
        num = (jnp.dot(s.astype(BF16), vb, preferred_element_type=F32)
               + e_inter * jnp.dot(qb, C_old.astype(BF16), preferred_element_type=F32))
        den = (jnp.sum(s, axis=-1, keepdims=True)
               + e_inter * jnp.sum(q * n_old, axis=-1, keepdims=True))
        hval = num / jnp.maximum(jnp.abs(den), jnp.exp(-m_t))

        g_col = b_last - b_col + ig_col
        g_row = b_last - b_row + ig_row
        m_new = jnp.maximum(b_last + m_prev, jnp.max(g_row, axis=-1, keepdims=True))
        wk = jnp.exp(g_col - m_new)
        decay = jnp.exp(b_last + m_prev - m_new)
        kw = k * wk
        C_ref[h] = decay * C_old + lax.dot_general(kw.astype(BF16), vb, (((0,), (0,)), ((), ())),
                                                   preferred_element_type=F32)
        n_ref[h] = decay * n_old + jnp.sum(kw, axis=0, keepdims=True)
        m_ref[h] = jnp.broadcast_to(m_new, m_ref.shape[1:])

        hnorm = hval * lax.rsqrt(jnp.mean(hval * hval, axis=-1, keepdims=True) + EPS)
        ym = hnorm * ng_ref[:, hs] * jax.nn.sigmoid(o_ref[0, :, hs])
        y_ref[0, :, hs] = ym.astype(BF16)


def _mlstm(proj3, gates3, gates_t, bg_row, bg_col, conv_qk, mh_norm_g, L):
    B, S, _ = proj3.shape
    MW = mh_norm_g.shape[-1]
    dh = MW // M_HEADS
    nc = S // L
    l8 = L // SUBLANES

    def prev_map(colblk):
        return lambda b, c: (b, jnp.maximum(c * l8 - 1, 0), colblk)

    def cur_map(colblk):
        return lambda b, c: (b, c, colblk)

    return pl.pallas_call(
        _mlstm_kernel,
        grid=(B, nc),
        in_specs=[pl.BlockSpec((1, SUBLANES, MW), prev_map(0)),
                  pl.BlockSpec((1, SUBLANES, MW), prev_map(1)),
                  pl.BlockSpec((1, L, MW), cur_map(0)),
                  pl.BlockSpec((1, L, MW), cur_map(1)),
                  pl.BlockSpec((1, L, MW), cur_map(2)),
                  pl.BlockSpec((1, L, MW), cur_map(3)),
                  pl.BlockSpec((1, L, LANES), lambda b, c: (b, c, 0)),
                  pl.BlockSpec((SUBLANES, L), lambda b, c: (0, b * nc + c)),
                  pl.BlockSpec((1, LANES), lambda b, c: (0, 0)),
                  pl.BlockSpec((SUBLANES, LANES), lambda b, c: (0, 0)),
                  pl.BlockSpec((CONV_WIDTH, 2 * MW), lambda b, c: (0, 0)),
                  pl.BlockSpec((1, MW), lambda b, c: (0, 0))],
        out_specs=pl.BlockSpec((1, L, MW), lambda b, c: (b, c, 0)),
        out_shape=jax.ShapeDtypeStruct((B, S, MW), BF16),
        scratch_shapes=[pltpu.VMEM((M_HEADS, dh, dh), F32),
                        pltpu.VMEM((M_HEADS, 1, dh), F32),
                        pltpu.VMEM((M_HEADS, SUBLANES, LANES), F32)],
        compiler_params=_cparams(("arbitrary", "arbitrary")),
        name="mlstm",
    )(proj3, proj3, proj3, proj3, proj3, proj3, gates3, gates_t, bg_row, bg_col, conv_qk,
      mh_norm_g)


def _lru_scan(a, u, h0):
    T, C = a.shape
    G = T // SUBLANES
    a = a.reshape(G, SUBLANES, C)
    u = u.reshape(G, SUBLANES, C)
    rid = lax.broadcasted_iota(I32, a.shape, 1)
    k = 1
    while k < SUBLANES:
        keep = rid >= k
        a_sh = jnp.where(keep, pltpu.roll(a, k, axis=1), 1.0)
        u_sh = jnp.where(keep, pltpu.roll(u, k, axis=1), 0.0)
        u = a * u_sh + u
        a = a * a_sh
        k *= 2
    h = h0
    groups = []
    for g in range(G):
        blk = u[g] + a[g] * h
        groups.append(blk)
        h = blk[SUBLANES - 1:SUBLANES, :]
    return jnp.concatenate(groups, axis=0)


def _rglru_kernel(xp_ref, x_ref, gr_ref, cw_ref, cb_ref, wax_ref, ba_ref, bx_ref, lam_ref, ng_ref,
                  y_ref, h_ref):
    t = pl.program_id(1)
    T = x_ref.shape[1]
    RW = x_ref.shape[2]
    bd = RW // R_BLOCKS

    @pl.when(t == 0)
    def _():
        h_ref[...] = jnp.zeros_like(h_ref)

    prev = jnp.where(t > 0, xp_ref[0], 0.0)
    xr = _causal_conv(x_ref[0], prev, cw_ref[...]) + cb_ref[...]
    xrb = xr.astype(BF16)
    ls = _log_sigmoid(lam_ref[...])
    for n in range(R_BLOCKS):
        sl = slice(n * bd, (n + 1) * bd)
        z = jnp.dot(xrb[:, sl], wax_ref[n], preferred_element_type=F32)
        r_gate = jax.nn.sigmoid(z[:, :bd] + ba_ref[:, sl])
        i_gate = jax.nn.sigmoid(z[:, bd:] + bx_ref[:, sl])
        log_a = LRU_C * r_gate * ls[:, sl]
        a = jnp.exp(log_a)
        u = jnp.sqrt(-jnp.tanh(log_a) * (a * a + 1.0)) * (i_gate * xr[:, sl])
        hseq = _lru_scan(a, u, h_ref[:, sl])
        h_ref[:, sl] = hseq[T - 1:T, :]
        y = hseq * jax.nn.gelu(gr_ref[0, :, sl])
        y = y * lax.rsqrt(jnp.mean(y * y, axis=-1, keepdims=True) + EPS) * ng_ref[:, sl]
        y_ref[0, :, sl] = y.astype(BF16)


def _rglru(proj3, lru_conv_w, lru_conv_b, wax, b_a, b_x, lam, ng, T, xr_blk, gr_blk):
    B, S, _ = proj3.shape
    RW = lam.shape[-1]
    bd = RW // R_BLOCKS
    t8 = T // SUBLANES
    vec = pl.BlockSpec((1, RW), lambda b, t: (0, 0))
    return pl.pallas_call(
        _rglru_kernel,
        grid=(B, S // T),
        in_specs=[pl.BlockSpec((1, SUBLANES, RW), lambda b, t: (b, jnp.maximum(t * t8 - 1, 0), xr_blk)),
                  pl.BlockSpec((1, T, RW), lambda b, t: (b, t, xr_blk)),
                  pl.BlockSpec((1, T, RW), lambda b, t: (b, t, gr_blk)),
                  pl.BlockSpec((CONV_WIDTH, RW), lambda b, t: (0, 0)),
                  vec,
                  pl.BlockSpec((R_BLOCKS, bd, 2 * bd), lambda b, t: (0, 0, 0)),
                  vec, vec, vec, vec],
        out_specs=pl.BlockSpec((1, T, RW), lambda b, t: (b, t, 0)),
        out_shape=jax.ShapeDtypeStruct((B, S, RW), BF16),
        scratch_shapes=[pltpu.VMEM((1, RW), F32)],
        compiler_params=_cparams(("arbitrary", "arbitrary")),
        name="rglru",
    )(proj3, proj3, proj3, lru_conv_w, lru_conv_b, wax, b_a, b_x, lam, ng)


def _rows_to_token_tiles(x, stage_ref, tiles_ref):
    T, D = x.shape
    nq = D // LANES
    for q in range(nq):
        stage_ref[pl.ds(q, T, stride=nq), :] = x[:, q * LANES:(q + 1) * LANES]

    def per_token(t, carry):
        tiles_ref[t] = stage_ref[pl.ds(pl.multiple_of(t * nq, nq), nq), :].astype(BF16)
        return carry

    lax.fori_loop(0, T, per_token, 0, unroll=8)


def _token_tiles_to_rows(tiles_ref, stage_ref):
    T, nq, _ = tiles_ref.shape

    def per_token(t, carry):
        stage_ref[pl.ds(pl.multiple_of(t * nq, nq), nq), :] = tiles_ref[t].astype(F32)
        return carry

    lax.fori_loop(0, T, per_token, 0, unroll=8)
    return jnp.concatenate([stage_ref[pl.ds(q, T, stride=nq), :].astype(BF16) for q in range(nq)],
                           axis=1)


def _outproj_kernel(x_ref, ym_ref, yr_ref, mod_ref, wm_ref, wr_ref, wrt_ref, brt_ref,
                    x1_ref, hp_ref, lt_ref, stage_ref):
    mix = (jnp.dot(ym_ref[...], wm_ref[...], preferred_element_type=F32)
           + jnp.dot(yr_ref[...], wr_ref[...], preferred_element_type=F32))
    x1 = x_ref[...] + mod_ref[0, 2:3, :] * mix
    x1_ref[...] = x1
    ms = jnp.mean(x1 * x1, axis=-1, keepdims=True)
    hn = x1 * lax.rsqrt(ms + EPS) * (1.0 + mod_ref[0, 4:5, :]) + mod_ref[0, 3:4, :]
    lt_ref[...] = lax.dot_general(wrt_ref[...], hn.astype(BF16), _NT,
                                  preferred_element_type=F32) + brt_ref[:, 0:1]
    _rows_to_token_tiles(hn, stage_ref, hp_ref)


def _out_proj(x2, ym2, yr2, mod3, w_out_b, w_rt_t, b_rt, S, tm):
    N, D = x2.shape
    MW = ym2.shape[1]
    RW = yr2.shape[1]
    assert MW == RW, "the two head groups share one row-block size of w_out"
    NR = w_rt_t.shape[0]
    spb = S // tm
    return pl.pallas_call(
        _outproj_kernel,
        grid=(N // tm,),
        in_specs=[pl.BlockSpec((tm, D), lambda i: (i, 0)),
                  pl.BlockSpec((tm, MW), lambda i: (i, 0)),
                  pl.BlockSpec((tm, RW), lambda i: (i, 0)),
                  pl.BlockSpec((1, 6, D), lambda i: (i // spb, 0, 0)),
                  pl.BlockSpec((MW, D), lambda i: (0, 0)),
                  pl.BlockSpec((RW, D), lambda i: (1, 0)),
                  pl.BlockSpec((NR, D), lambda i: (0, 0)),
                  pl.BlockSpec((NR, LANES), lambda i: (0, 0))],
        out_specs=[pl.BlockSpec((tm, D), lambda i: (i, 0)),
                   pl.BlockSpec((tm, D // LANES, LANES), lambda i: (i, 0, 0)),
                   pl.BlockSpec((NR, tm), lambda i: (0, i))],
        out_shape=[jax.ShapeDtypeStruct((N, D), F32),
                   jax.ShapeDtypeStruct((N, D // LANES, LANES), BF16),
                   jax.ShapeDtypeStruct((NR, N), F32)],
        scratch_shapes=[pltpu.VMEM((tm * (D // LANES), LANES), F32)],
        compiler_params=_cparams(("arbitrary",)),
        name="out_proj",
    )(x2, ym2, yr2, mod3, w_out_b, w_out_b, w_rt_t, b_rt)


def _router_kernel(lt_ref, ri_ref, rw_ref, meta_ref, pcol_ref, carry_ref):
    i = pl.program_id(0)
    tm = lt_ref.shape[1]
    E8 = EXPERTS_PER_GROUP

    @pl.when(i == 0)
    def _():
        carry_ref[...] = jnp.zeros_like(carry_ref)
        meta_ref[...] = jnp.zeros_like(meta_ref)
        pcol_ref[...] = jnp.zeros_like(pcol_ref)

    sub = lax.broadcasted_iota(I32, (SUBLANES, tm), 0)
    gl = jnp.where(sub < N_GROUPS, lt_ref[N_EXPERTS:N_EXPERTS + SUBLANES, :], -jnp.inf)
    ge = jnp.exp(gl - jnp.max(gl, axis=0, keepdims=True))
    pg = ge / jnp.sum(ge, axis=0, keepdims=True)
    pg_sel = jnp.max(pg, axis=0, keepdims=True)
    g_sel = jnp.min(jnp.where(pg == pg_sel, sub, SUBLANES), axis=0, keepdims=True)

    el = lt_ref[(N_GROUPS - 1) * E8:N_GROUPS * E8, :]
    for g in range(N_GROUPS - 2, -1, -1):
        el = jnp.where(g_sel == g, lt_ref[g * E8:(g + 1) * E8, :], el)
    ee = jnp.exp(el - jnp.max(el, axis=0, keepdims=True))
    pe = ee / jnp.sum(ee, axis=0, keepdims=True)
    p0 = jnp.max(pe, axis=0, keepdims=True)
    i0 = jnp.min(jnp.where(pe == p0, sub, SUBLANES), axis=0, keepdims=True)
    pe1 = jnp.where(sub == i0, -1.0, pe)
    p1 = jnp.max(pe1, axis=0, keepdims=True)
    i1 = jnp.min(jnp.where(pe1 == p1, sub, SUBLANES), axis=0, keepdims=True)
    psum = p0 + p1
    w0 = pg_sel * p0 / psum
    w1 = pg_sel * p1 / psum
    e0 = g_sel * E8 + i0
    e1 = g_sel * E8 + i1

    eid = lax.broadcasted_iota(I32, (N_EXPERTS, tm), 0)
    oh0 = (eid == e0).astype(F32)
    oh1 = (eid == e1).astype(F32)
    oh = oh0 + oh1
    r_ = lax.broadcasted_iota(I32, (tm, tm), 0)
    c_ = lax.broadcasted_iota(I32, (tm, tm), 1)
    before = (r_ < c_).astype(BF16)
    cnt = carry_ref[:, 0:1] + jnp.dot(oh.astype(BF16), before, preferred_element_type=F32)
    pos0 = jnp.sum(oh0 * cnt, axis=0, keepdims=True)
    pos1 = jnp.sum(oh1 * cnt, axis=0, keepdims=True)
    new_carry = carry_ref[...] + jnp.sum(oh, axis=1, keepdims=True)
    carry_ref[...] = new_carry

    ri = jnp.where(sub == 0, e0, 0)
    ri = jnp.where(sub == 1, e1, ri)
    ri = jnp.where(sub == 2, pos0.astype(I32), ri)
    ri = jnp.where(sub == 3, pos1.astype(I32), ri)
    ri_ref[...] = ri
    rw_ref[...] = jnp.where(sub == 0, w0, jnp.where(sub == 1, w1, 0.0))

    @pl.when(i == pl.num_programs(0) - 1)
    def _():
        nbp = meta_ref.shape[1]
        cnt_col = new_carry[:, 0:1]
        padded_col = jnp.ceil(cnt_col / MOE_ROWS) * MOE_ROWS
        nonempty = cnt_col > 0.0
        e_sub = lax.broadcasted_iota(I32, (N_EXPERTS, nbp), 0)
        lane = lax.broadcasted_iota(I32, (N_EXPERTS, nbp), 1)
        padded_row = jnp.sum(jnp.where(e_sub == lane, padded_col, 0.0), axis=0, keepdims=True)
        pend_row = jnp.sum(jnp.where(e_sub <= lane, padded_col, 0.0), axis=0, keepdims=True)
        pend_col = jnp.sum(jnp.where(lane <= e_sub, padded_row, 0.0), axis=1, keepdims=True)
        blk_start = lane.astype(F32) * MOE_ROWS
        e_f = e_sub.astype(F32)
        be = jnp.sum(jnp.where(pend_col <= blk_start, 1.0, 0.0), axis=0, keepdims=True)
        be = jnp.minimum(be, N_EXPERTS - 1.0)
        nxt = jnp.min(jnp.where(jnp.logical_and(e_f > be, nonempty), e_f, float(N_EXPERTS)),
                      axis=0, keepdims=True)
        run = jnp.sum(jnp.where(jnp.logical_and(e_f < be, nonempty), 1.0, 0.0),
                      axis=0, keepdims=True)
        cnt_row = jnp.sum(jnp.where(e_sub == lane, cnt_col, 0.0), axis=0, keepdims=True)
        n_used = pend_row[:, N_EXPERTS - 1:N_EXPERTS] / MOE_ROWS
        sub8 = lax.broadcasted_iota(I32, (SUBLANES, nbp), 0)
        meta = jnp.where(sub8 == META_BLOCK_E, be, 0.0)
        meta = jnp.where(sub8 == META_NEXT_E, nxt, meta)
        meta = jnp.where(sub8 == META_SLOT, run - 2.0 * jnp.floor(run * 0.5), meta)
        meta = jnp.where(sub8 == META_NUSED, n_used, meta)
        meta = jnp.where(sub8 == META_PAD_LO, pend_row - padded_row + cnt_row, meta)
        meta = jnp.where(sub8 == META_PAD_N, padded_row - cnt_row, meta)
        meta_ref[...] = meta.astype(I32)
        pcol_ref[...] = jnp.broadcast_to(pend_col - padded_col, pcol_ref.shape)


def _router(logits_t, tm, n_blocks):
    NR, N = logits_t.shape
    nbp = max(-(-n_blocks // LANES), 1) * LANES
    return pl.pallas_call(
        _router_kernel,
        grid=(N // tm,),
        in_specs=[pl.BlockSpec((NR, tm), lambda i: (0, i))],
        out_specs=[pl.BlockSpec((SUBLANES, tm), lambda i: (0, i)),
                   pl.BlockSpec((SUBLANES, tm), lambda i: (0, i)),
                   pl.BlockSpec((SUBLANES, nbp), lambda i: (0, 0)),
                   pl.BlockSpec((N_EXPERTS, LANES), lambda i: (0, 0))],
        out_shape=[jax.ShapeDtypeStruct((SUBLANES, N), I32),
                   jax.ShapeDtypeStruct((SUBLANES, N), F32),
                   jax.ShapeDtypeStruct((SUBLANES, nbp), I32),
                   jax.ShapeDtypeStruct((N_EXPERTS, LANES), F32)],
        scratch_shapes=[pltpu.VMEM((N_EXPERTS, LANES), F32)],
        compiler_params=_cparams(("arbitrary",)),
        name="router",
    )(logits_t)


def _slots_kernel(ri_ref, pcol_ref, d_ref):
    tm = ri_ref.shape[1]
    eid = lax.broadcasted_iota(I32, (N_EXPERTS, tm), 0)
    pstart = pcol_ref[:, 0:1]
    sub = lax.broadcasted_iota(I32, (SUBLANES, tm), 0)
    out = jnp.zeros((SUBLANES, tm), I32)
    for kk in range(TOP_K):
        first = jnp.sum(jnp.where(eid == ri_ref[kk:kk + 1, :], pstart, 0.0), axis=0, keepdims=True)
        out = jnp.where(sub == kk, first.astype(I32) + ri_ref[TOP_K + kk:TOP_K + kk + 1, :], out)
    d_ref[...] = out


def _slots(ri, pcol, tm):
    _, N = ri.shape
    return pl.pallas_call(
        _slots_kernel,
        grid=(N // tm,),
        in_specs=[pl.BlockSpec((SUBLANES, tm), lambda i: (0, i)),
                  pl.BlockSpec((N_EXPERTS, LANES), lambda i: (0, 0))],
        out_specs=pl.BlockSpec((SUBLANES, tm), lambda i: (0, i)),
        out_shape=jax.ShapeDtypeStruct((SUBLANES, N), I32),
        compiler_params=_cparams(("arbitrary",)),
        name="slots",
    )(ri, pcol)


ROW_DMA_UNROLL = 8


def _row_copy(src, dst, sem):
    return pltpu.make_async_copy(src, dst, sem)


def _unrolled(n, fn):
    def trip(g, carry):
        for u in range(ROW_DMA_UNROLL):
            fn(g * ROW_DMA_UNROLL + u)
        return carry
    lax.fori_loop(0, n // ROW_DMA_UNROLL, trip, 0)


def _dispatch_kernel(meta_ref, dest_ref, hp_ref, xs_ref, zero_ref, sem):
    i = pl.program_id(0)
    tm = hp_ref.shape[0]
    n_blocks = xs_ref.shape[0] // MOE_ROWS

    def issue(t):
        for kk in range(TOP_K):
            _row_copy(hp_ref.at[t], xs_ref.at[dest_ref[kk, t]], sem.at[0]).start(priority=kk)

    _unrolled(tm, issue)
    _unrolled(TOP_K * tm, lambda t: _row_copy(hp_ref.at[0], xs_ref.at[0], sem.at[0]).wait())

    @pl.when(i == pl.num_programs(0) - 1)
    def _():
        zero_ref[...] = jnp.zeros_like(zero_ref)
        for e in range(N_EXPERTS):
            lo = meta_ref[META_PAD_LO, e]
            n_pad = meta_ref[META_PAD_N, e]

            def fill(r, carry):
                _row_copy(zero_ref.at[0], xs_ref.at[lo + r], sem.at[1]).start()
                return carry

            def drain(r, carry):
                _row_copy(zero_ref.at[0], xs_ref.at[0], sem.at[1]).wait()
                return carry

            lax.fori_loop(0, n_pad, fill, 0)
            lax.fori_loop(0, n_pad, drain, 0)

        def block_copy(b):
            return _row_copy(zero_ref, xs_ref.at[pl.ds(b * MOE_ROWS, MOE_ROWS)], sem.at[1])

        n_used = meta_ref[META_NUSED, 0]
        lax.fori_loop(n_used, n_blocks, lambda b, c: (block_copy(b).start(), c)[1], 0)
        lax.fori_loop(n_used, n_blocks, lambda b, c: (block_copy(0).wait(), c)[1], 0)


def _dispatch(meta, dest, hp, n_slots, tm):
    N, nq, _ = hp.shape
    grid_spec = pltpu.PrefetchScalarGridSpec(
        num_scalar_prefetch=1,
        grid=(N // tm,),
        in_specs=[pl.BlockSpec((SUBLANES, tm), lambda i, m: (0, i), memory_space=pltpu.SMEM),
                  pl.BlockSpec((tm, nq, LANES), lambda i, m: (i, 0, 0))],
        out_specs=pl.BlockSpec(memory_space=pl.ANY),
        scratch_shapes=[pltpu.VMEM((MOE_ROWS, nq, LANES), BF16), pltpu.SemaphoreType.DMA((2,))],
    )
    return pl.pallas_call(
        _dispatch_kernel,
        grid_spec=grid_spec,
        out_shape=jax.ShapeDtypeStruct((n_slots, nq, LANES), BF16),
        compiler_params=_cparams(("arbitrary",)),
        name="dispatch",
    )(meta, dest, hp)


def _expert_kernel(meta_ref, xs_ref, wg_hbm, wu_hbm, wd_hbm, ys_ref, wbuf_g, wbuf_u, wbuf_d,
                   wgu_s, wd_s, stage_ref, sem):
    i = pl.program_id(0)
    DE = wbuf_g.shape[2]
    e = meta_ref[META_BLOCK_E, i]
    e_prev = meta_ref[META_BLOCK_E, jnp.maximum(i - 1, 0)]
    used = i < meta_ref[META_NUSED, 0]
    first = jnp.logical_and(used, jnp.logical_or(i == 0, e != e_prev))
    slot = meta_ref[META_SLOT, i]
    nxt = meta_ref[META_NEXT_E, i]

    def weight_copies(ex, s):
        return (_row_copy(wg_hbm.at[ex], wbuf_g.at[s], sem.at[s]),
                _row_copy(wu_hbm.at[ex], wbuf_u.at[s], sem.at[s]),
                _row_copy(wd_hbm.at[ex], wbuf_d.at[s], sem.at[s]))

    @pl.when(jnp.logical_and(used, i == 0))
    def _():
        for cp in weight_copies(e, slot):
            cp.start()

    @pl.when(first)
    def _():
        @pl.when(nxt < N_EXPERTS)
        def _():
            for cp in weight_copies(nxt, 1 - slot):
                cp.start()

        for cp in weight_copies(e, slot):
            cp.wait()
        wgu_s[:, :DE] = wbuf_g[slot].astype(BF16)
        wgu_s[:, DE:] = wbuf_u[slot].astype(BF16)
        wd_s[...] = wbuf_d[slot].astype(BF16)

    @pl.when(used)
    def _():
        xb = _token_tiles_to_rows(xs_ref, stage_ref)
        gu = jnp.dot(xb, wgu_s[...], preferred_element_type=F32)
        hb = _silu(gu[:, :DE]) * gu[:, DE:]
        ys_ref[...] = jnp.dot(hb.astype(BF16), wd_s[...], preferred_element_type=F32)

    @pl.when(jnp.logical_not(used))
    def _():
        ys_ref[...] = jnp.zeros_like(ys_ref)


def _experts(meta, xs, w_e_gate, w_e_up, w_e_down):
    n_slots, nq, _ = xs.shape
    _, D, DE = w_e_gate.shape
    n_blocks = n_slots // MOE_ROWS
    hbm = pl.BlockSpec(memory_space=pl.ANY)
    grid_spec = pltpu.PrefetchScalarGridSpec(
        num_scalar_prefetch=1,
        grid=(n_blocks,),
        in_specs=[pl.BlockSpec((MOE_ROWS, nq, LANES), lambda i, m: (i, 0, 0)), hbm, hbm, hbm],
        out_specs=pl.BlockSpec((MOE_ROWS, D), lambda i, m: (i, 0)),
        scratch_shapes=[pltpu.VMEM((2, D, DE), F32), pltpu.VMEM((2, D, DE), F32),
                        pltpu.VMEM((2, DE, D), F32),
                        pltpu.VMEM((D, 2 * DE), BF16), pltpu.VMEM((DE, D), BF16),
                        pltpu.VMEM((MOE_ROWS * nq, LANES), F32),
                        pltpu.SemaphoreType.DMA((2,))],
    )
    return pl.pallas_call(
        _expert_kernel,
        grid_spec=grid_spec,
        out_shape=jax.ShapeDtypeStruct((n_slots, D), F32),
        compiler_params=_cparams(("arbitrary",)),
        name="experts",
    )(meta, xs, w_e_gate, w_e_up, w_e_down)


def _combine_kernel(rcur_ref, rnext_ref, x1_ref, wt_ref, mod_ref, fg_ref, ys_ref, o_ref, gbuf, sem):
    i = pl.program_id(0)
    n = pl.num_programs(0)
    tm = x1_ref.shape[0]
    slot = i % 2

    def gather(dest_ref, s):
        def issue(t):
            for kk in range(TOP_K):
                _row_copy(ys_ref.at[pl.ds(dest_ref[kk, t], 1)], gbuf.at[s, kk, pl.ds(t, 1)],
                          sem.at[s]).start(priority=kk)
        _unrolled(tm, issue)

    @pl.when(i == 0)
    def _():
        gather(rcur_ref, 0)

    @pl.when(i + 1 < n)
    def _():
        gather(rnext_ref, 1 - slot)

    _unrolled(TOP_K * tm,
              lambda t: _row_copy(ys_ref.at[pl.ds(0, 1)], gbuf.at[slot, 0, pl.ds(0, 1)],
                                  sem.at[slot]).wait())

    wt = wt_ref[...]
    y = wt[:, 0:1] * gbuf[slot, 0] + wt[:, 1:2] * gbuf[slot, 1]
    x2 = x1_ref[...] + mod_ref[0, 5:6, :] * y
    ms = jnp.mean(x2 * x2, axis=-1, keepdims=True)
    o_ref[...] = x2 * lax.rsqrt(ms + EPS) * fg_ref[...]


def _combine(dest, x1, wts, mod3, final_g, ys, S, tm):
    N, D = x1.shape
    spb = S // tm
    nsteps = N // tm
    return pl.pallas_call(
        _combine_kernel,
        grid=(nsteps,),
        in_specs=[pl.BlockSpec((SUBLANES, tm), lambda i: (0, i), memory_space=pltpu.SMEM),
                  pl.BlockSpec((SUBLANES, tm), lambda i: (0, jnp.minimum(i + 1, nsteps - 1)),
                               memory_space=pltpu.SMEM),
                  pl.BlockSpec((tm, D), lambda i: (i, 0)),
                  pl.BlockSpec((tm, SUBLANES), lambda i: (i, 0)),
                  pl.BlockSpec((1, 6, D), lambda i: (i // spb, 0, 0)),
                  pl.BlockSpec((1, D), lambda i: (0, 0)),
                  pl.BlockSpec(memory_space=pl.ANY)],
        out_specs=pl.BlockSpec((tm, D), lambda i: (i, 0)),
        out_shape=jax.ShapeDtypeStruct((N, D), F32),
        scratch_shapes=[pltpu.VMEM((2, TOP_K, tm, D), F32), pltpu.SemaphoreType.DMA((2,))],
        compiler_params=_cparams(("arbitrary",)),
        name="combine",
    )(dest, dest, x1, wts, mod3, final_g.reshape(1, D), ys)


def _tiles(S):
    def fit(t):
        return min(t, S)
    return dict(prep_tr=256, inproj_tm=fit(1024), inproj_tn=1024, chunk=fit(256), lru_t=fit(256),
                outproj_tm=fit(256), router_tm=fit(512), dispatch_tm=fit(256), combine_tm=fit(256))


def _layer(x, mod, w_in, b_gates, conv_qk, mh_norm_g, lru_conv_w, lru_conv_b, w_lru_a, b_lru_a,
           w_lru_x, b_lru_x, lru_lambda, lru_norm_g, w_out, w_group, b_group, w_router, b_router,
           w_e_gate, w_e_up, w_e_down, final_g):
    B, S, D = x.shape
    N = B * S
    MW = mh_norm_g.shape[-1]
    RW = lru_lambda.shape[-1]
    H = M_HEADS
    tl = _tiles(S)
    x2 = x.reshape(N, D)
    mod3 = mod.reshape(B, 6, D)

    n_qkvo = 4 * MW
    w_in_t = w_in.T
    w_main_t = _prep_w_in(w_in_t, n_qkvo, tl["prep_tr"])
    w_gate_t = w_in_t[n_qkvo:n_qkvo + 2 * H]
    w_ift = jnp.zeros((SUBLANES, D), F32).at[:2 * H].set(w_gate_t).astype(BF16)
    w_if_t = jnp.zeros((LANES, D), F32).at[:2 * H].set(w_gate_t).astype(BF16)
    proj, gates, gates_t = _in_proj(x2, mod3, w_main_t, w_if_t, w_ift, S, tl["inproj_tm"],
                                    tl["inproj_tn"])
    proj3 = proj.reshape(B, S, -1)

    bg_row = jnp.zeros((1, LANES), F32).at[0, :2 * H].set(b_gates)
    bg_col = jnp.zeros((SUBLANES, LANES), F32).at[:2 * H, :].set(b_gates[:, None])
    ym = _mlstm(proj3, gates.reshape(B, S, LANES), gates_t, bg_row, bg_col, conv_qk,
                mh_norm_g.reshape(1, MW), tl["chunk"])

    wax = jnp.concatenate([w_lru_a, w_lru_x], axis=-1).astype(BF16)
    yr = _rglru(proj3, lru_conv_w, lru_conv_b.reshape(1, RW), wax, b_lru_a.reshape(1, RW),
                b_lru_x.reshape(1, RW), lru_lambda.reshape(1, RW), lru_norm_g.reshape(1, RW),
                tl["lru_t"], n_qkvo // RW, n_qkvo // RW + 1)

    NR = N_EXPERTS + SUBLANES
    w_rt_t = (jnp.zeros((NR, D), F32).at[:N_EXPERTS].set(w_router.T)
              .at[N_EXPERTS:N_EXPERTS + N_GROUPS].set(w_group.T)).astype(BF16)
    b_rt = (jnp.zeros((NR, LANES), F32).at[:N_EXPERTS, :].set(b_router[:, None])
            .at[N_EXPERTS:N_EXPERTS + N_GROUPS, :].set(b_group[:, None]))
    x1, hp, logits_t = _out_proj(x2, ym.reshape(N, MW), yr.reshape(N, RW), mod3,
                                 w_out.astype(BF16), w_rt_t, b_rt, S, tl["outproj_tm"])

    n_slots = N * TOP_K + N_EXPERTS * MOE_ROWS
    ri, rw, meta, pcol = _router(logits_t, tl["router_tm"], n_slots // MOE_ROWS)
    dest = _slots(ri, pcol, tl["router_tm"])
    xs = _dispatch(meta, dest, hp, n_slots, tl["dispatch_tm"])
    ys = _experts(meta, xs, w_e_gate, w_e_up, w_e_down)
    out = _combine(dest, x1, rw.T, mod3, final_g, ys, S, tl["combine_tm"])
    return out.reshape(B, S, D)


def kernel(x, c, w_ada, b_ada, w_in, b_gates, conv_qk, mh_norm_g, lru_conv_w, lru_conv_b, w_lru_a, b_lru_a, w_lru_x, b_lru_x, lru_lambda, lru_norm_g, w_out, w_group, b_group, w_router, b_router, w_e_gate, w_e_up, w_e_down, final_g):
    depth = w_ada.shape[0]
    assert depth == 1, "single trunk layer"
    l = 0
    mod = _ada_mod(c, w_ada[l], b_ada[l])
    return _layer(x, mod, w_in[l], b_gates[l], conv_qk[l], mh_norm_g[l], lru_conv_w[l],
                  lru_conv_b[l], w_lru_a[l], b_lru_a[l], w_lru_x[l], b_lru_x[l], lru_lambda[l],
                  lru_norm_g[l], w_out[l], w_group[l], b_group[l], w_router[l], b_router[l],
                  w_e_gate[l], w_e_up[l], w_e_down[l], final_g)
```

```python
import functools

import jax
import jax.numpy as jnp
from jax import lax
from jax.experimental import pallas as pl
from jax.experimental.pallas import tpu as pltpu

F32 = jnp.float32
BF16 = jnp.bfloat16
U32 = jnp.uint32
I32 = jnp.int32

EPS = 1e-6
M_HEADS = 4
R_BLOCKS = 8
CONV_WIDTH = 4
LRU_C = 8.0
N_GROUPS = 4
EXPERTS_PER_GROUP = 8
N_EXPERTS = N_GROUPS * EXPERTS_PER_GROUP
TOP_K = 2

LANES = 128
SUBLANES = 8
VMEM_LIMIT = 56 * 1024 * 1024
VMEM_LIMIT_LARGE = 60 * 1024 * 1024
PROLOGUE_ROWS = 256

MOE_ROWS = 256
META_BLOCK_E, META_NEXT_E, META_SLOT, META_NUSED, META_PAD_LO, META_PAD_N = range(6)


def _cparams(sem, vmem=VMEM_LIMIT):
    return pltpu.CompilerParams(dimension_semantics=sem, vmem_limit_bytes=vmem)


def _log_sigmoid(x):
    return jnp.minimum(x, 0.0) - jnp.log1p(jnp.exp(-jnp.abs(x)))


def _silu(x):
    return x * jax.nn.sigmoid(x)


def _split3(x):
    hi = x.astype(BF16)
    r1 = x - hi.astype(F32)
    mid = r1.astype(BF16)
    lo = (r1 - mid.astype(F32)).astype(BF16)
    return hi, mid, lo


def _causal_conv(cur, prev8, w4):
    T, C = cur.shape
    G = T // SUBLANES
    x3 = cur.reshape(G, SUBLANES, C)
    p3 = prev8.reshape(1, SUBLANES, C)
    rid = lax.broadcasted_iota(I32, x3.shape, 1)
    acc = x3 * w4[CONV_WIDTH - 1:CONV_WIDTH, :]
    for d in range(1, CONV_WIDTH):
        rot = pltpu.roll(x3, d, axis=1)
        before = jnp.concatenate([pltpu.roll(p3, d, axis=1), rot[:G - 1]], axis=0)
        acc = acc + jnp.where(rid < d, before, rot) * w4[CONV_WIDTH - 1 - d:CONV_WIDTH - d, :]
    return acc.reshape(T, C)


def _ada_kernel(c_ref, w_ref, b_ref, o_ref):
    s = _silu(c_ref[...])
    o_ref[...] = jnp.dot(s.astype(BF16), w_ref[...].astype(BF16),
                         preferred_element_type=F32) + b_ref[...]


def _ada_mod(c, w_ada, b_ada):
    B, D = c.shape
    n6 = w_ada.shape[1]
    tn = 1024
    cp = jnp.zeros((SUBLANES, D), F32).at[:B].set(c)
    out = pl.pallas_call(
        _ada_kernel,
        grid=(n6 // tn,),
        in_specs=[pl.BlockSpec((SUBLANES, D), lambda j: (0, 0)),
                  pl.BlockSpec((D, tn), lambda j: (0, j)),
                  pl.BlockSpec((1, tn), lambda j: (0, j))],
        out_specs=pl.BlockSpec((SUBLANES, tn), lambda j: (0, j)),
        out_shape=jax.ShapeDtypeStruct((SUBLANES, n6), F32),
        compiler_params=_cparams(("arbitrary",)),
        name="ada_mod",
    )(cp, w_ada, b_ada.reshape(1, n6))
    return out[:B]


_NT = (((1,), (1,)), ((), ()))


def _prep_kernel(w_ref, wm_ref):
    wm_ref[...] = w_ref[...].astype(BF16)


def _prep_w_in(w_in_t, n_qkvo, tr):
    nin, D = w_in_t.shape
    n_gate = 2 * M_HEADS
    nm = nin - n_gate
    assert n_qkvo % tr == 0 and nm % tr == 0 and n_gate % SUBLANES == 0
    return pl.pallas_call(
        _prep_kernel,
        grid=(nm // tr,),
        in_specs=[pl.BlockSpec((pl.Element(tr), pl.Element(D)),
                               lambda i: (pl.multiple_of(
                                   i * tr + jnp.where(i * tr < n_qkvo, 0, n_gate), SUBLANES), 0))],
        out_specs=pl.BlockSpec((tr, D), lambda i: (i, 0)),
        out_shape=jax.ShapeDtypeStruct((nm, D), BF16),
        compiler_params=_cparams(("arbitrary",)),
        name="prep_w_in",
    )(w_in_t)


def _inproj_kernel(x_ref, mod_ref, w_ref, wif_ref, wift_ref, o_ref, g_ref, gt_ref, hn_ref):
    j = pl.program_id(1)

    @pl.when(j == 0)
    def _():
        tm = x_ref.shape[0]
        rc = min(tm, PROLOGUE_ROWS)
        for r in range(tm // rc):
            rows = slice(r * rc, (r + 1) * rc)
            x = x_ref[rows, :]
            ms = jnp.mean(x * x, axis=-1, keepdims=True)
            hn = x * lax.rsqrt(ms + EPS) * (1.0 + mod_ref[0, 1:2, :]) + mod_ref[0, 0:1, :]
            hb = hn.astype(BF16)
            hn_ref[rows, :] = hb
            g_ref[rows, :] = lax.dot_general(hb, wif_ref[...], _NT, preferred_element_type=F32)
            gt_ref[:, rows] = lax.dot_general(wift_ref[...], hb, _NT, preferred_element_type=F32)

    o_ref[...] = lax.dot_general(hn_ref[...], w_ref[...], _NT, preferred_element_type=F32)


def _in_proj(x2, mod3, w_main_t, w_if_t, w_ift, S, tm, tn):
    N, D = x2.shape
    nw = w_main_t.shape[0]
    spb = S // tm
    return pl.pallas_call(
        _inproj_kernel,
        grid=(N // tm, nw // tn),
        in_specs=[pl.BlockSpec((tm, D), lambda i, j: (i, 0)),
                  pl.BlockSpec((1, 6, D), lambda i, j: (i // spb, 0, 0)),
                  pl.BlockSpec((tn, D), lambda i, j: (j, 0)),
                  pl.BlockSpec((LANES, D), lambda i, j: (0, 0)),
                  pl.BlockSpec((SUBLANES, D), lambda i, j: (0, 0))],
        out_specs=[pl.BlockSpec((tm, tn), lambda i, j: (i, j)),
                   pl.BlockSpec((tm, LANES), lambda i, j: (i, 0)),
                   pl.BlockSpec((SUBLANES, tm), lambda i, j: (0, i))],
        out_shape=[jax.ShapeDtypeStruct((N, nw), F32),
                   jax.ShapeDtypeStruct((N, LANES), F32),
                   jax.ShapeDtypeStruct((SUBLANES, N), F32)],
        scratch_shapes=[pltpu.VMEM((tm, D), BF16)],
        compiler_params=_cparams(("arbitrary", "arbitrary"), vmem=VMEM_LIMIT_LARGE),
        name="in_proj",
    )(x2, mod3, w_main_t, w_if_t, w_ift)


def _mlstm_kernel(qp_ref, kp_ref, q_ref, k_ref, v_ref, o_ref, g_ref, gt_ref, bgr_ref, bgc_ref,
                  cw_ref, ng_ref, y_ref, C_ref, n_ref, m_ref):
    c = pl.program_id(1)
    L = q_ref.shape[1]
    MW = q_ref.shape[2]
    dh = MW // M_HEADS
    H = M_HEADS

    @pl.when(c == 0)
    def _():
        C_ref[...] = jnp.zeros_like(C_ref)
        n_ref[...] = jnp.zeros_like(n_ref)
        m_ref[...] = jnp.zeros_like(m_ref)

    row = lax.broadcasted_iota(I32, (L, L), 0)
    col = lax.broadcasted_iota(I32, (L, L), 1)
    causal = col <= row
    tri = causal.astype(BF16)
    tri_t = (row <= col).astype(BF16)

    gb = g_ref[0] + bgr_ref[...]
    gtb = gt_ref[...] + bgc_ref[:, 0:1]
    b_col_all = sum(jnp.dot(tri, p, preferred_element_type=F32) for p in _split3(_log_sigmoid(gb)))
    b_row_all = sum(jnp.dot(p, tri_t, preferred_element_type=F32) for p in _split3(_log_sigmoid(gtb)))

    has_prev = c > 0

    for h in range(H):
        hs = slice(h * dh, (h + 1) * dh)
        ig_col = gb[:, h:h + 1]
        b_col = b_col_all[:, H + h:H + h + 1]
        ig_row = gtb[h:h + 1, :]
        b_row = b_row_all[H + h:H + h + 1, :]
        b_last = b_row[:, L - 1:L]
        m_prev = m_ref[h][0:1, 0:1]

        qprev = jnp.where(has_prev, qp_ref[0, :, hs], 0.0)
        kprev = jnp.where(has_prev, kp_ref[0, :, hs], 0.0)
        q = _silu(_causal_conv(q_ref[0, :, hs], qprev, cw_ref[:, hs]))
        k = _silu(_causal_conv(k_ref[0, :, hs], kprev,
                               cw_ref[:, MW + h * dh:MW + (h + 1) * dh])) * (dh ** -0.5)
        qb = q.astype(BF16)
        kb = k.astype(BF16)
        vb = v_ref[0, :, hs].astype(BF16)

        qk = lax.dot_general(qb, kb, (((1,), (1,)), ((), ())), preferred_element_type=F32)
        dmat = jnp.where(causal, b_col - b_row + ig_row, -jnp.inf)
        inter = b_col + m_prev
        m_t = jnp.maximum(inter, jnp.max(dmat, axis=-1, keepdims=True))
        s = qk * jnp.exp(dmat - m_t)
        e_inter = jnp.exp(inter - m_t)
        C_old = C_ref[h]
        n_old = n_ref[h]
        num = (jnp.dot(s.astype(BF16), vb, preferred_element_type=F32)
               + e_inter * jnp.dot(qb, C_old.astype(BF16), preferred_element_type=F32))
        den = (jnp.sum(s, axis=-1, keepdims=True)
               + e_inter * jnp.sum(q * n_old, axis=-1, keepdims=True))
        hval = num / jnp.maximum(jnp.abs(den), jnp.exp(-m_t))

        g_col = b_last - b_col + ig_col
        g_row = b_last - b_row + ig_row
        m_new = jnp.maximum(b_last + m_prev, jnp.max(g_row, axis=-1, keepdims=True))
        wk = jnp.exp(g_col - m_new)
        decay = jnp.exp(b_last + m_prev - m_new)
        kw = k * wk
        C_ref[h] = decay * C_old + lax.dot_general(kw.astype(BF16), vb, (((0,), (0,)), ((), ())),
                                                   preferred_element_type=F32)
        n_ref[h] = decay * n_old + jnp.sum(kw, axis=0, keepdims=True)
        m_ref[h] = jnp.broadcast_to(m_new, m_ref.shape[1:])

        hnorm = hval * lax.rsqrt(jnp.mean(hval * hval, axis=-1, keepdims=True) + EPS)
        ym = hnorm * ng_ref[:, hs] * jax.nn.sigmoid(o_ref[0, :, hs])
        y_ref[0, :, hs] = ym.astype(BF16)


def _mlstm(proj3, gates3, gates_t, bg_row, bg_col, conv_qk, mh_norm_g, L):
    B, S, _ = proj3.shape
    MW = mh_norm_g.shape[-1]
    dh = MW // M_HEADS
    nc = S // L
    l8 = L // SUBLANES

    def prev_map(colblk):
        return lambda b, c: (b, jnp.maximum(c * l8 - 1, 0), colblk)

    def cur_map(colblk):
        return lambda b, c: (b, c, colblk)

    return pl.pallas_call(
        _mlstm_kernel,
        grid=(B, nc),
        in_specs=[pl.BlockSpec((1, SUBLANES, MW), prev_map(0)),
                  pl.BlockSpec((1, SUBLANES, MW), prev_map(1)),
                  pl.BlockSpec((1, L, MW), cur_map(0)),
                  pl.BlockSpec((1, L, MW), cur_map(1)),
                  pl.BlockSpec((1, L, MW), cur_map(2)),
                  pl.BlockSpec((1, L, MW), cur_map(3)),
                  pl.BlockSpec((1, L, LANES), lambda b, c: (b, c, 0)),
                  pl.BlockSpec((SUBLANES, L), lambda b, c: (0, b * nc + c)),
                  pl.BlockSpec((1, LANES), lambda b, c: (0, 0)),
                  pl.BlockSpec((SUBLANES, LANES), lambda b, c: (0, 0)),
                  pl.BlockSpec((CONV_WIDTH, 2 * MW), lambda b, c: (0, 0)),
                  pl.BlockSpec((1, MW), lambda b, c: (0, 0))],
        out_specs=pl.BlockSpec((1, L, MW), lambda b, c: (b, c, 0)),
        out_shape=jax.ShapeDtypeStruct((B, S, MW), BF16),
        scratch_shapes=[pltpu.VMEM((M_HEADS, dh, dh), F32),
                        pltpu.VMEM((M_HEADS, 1, dh), F32),
                        pltpu.VMEM((M_HEADS, SUBLANES, LANES), F32)],
        compiler_params=_cparams(("arbitrary", "arbitrary")),
        name="mlstm",
    )(proj3, proj3, proj3, proj3, proj3, proj3, gates3, gates_t, bg_row, bg_col, conv_qk,
      mh_norm_g)


def _lru_scan(a, u, h0):
    T, C = a.shape
    G = T // SUBLANES
    a = a.reshape(G, SUBLANES, C)
    u = u.reshape(G, SUBLANES, C)
    rid = lax.broadcasted_iota(I32, a.shape, 1)
    k = 1
    while k < SUBLANES:
        keep = rid >= k
        a_sh = jnp.where(keep, pltpu.roll(a, k, axis=1), 1.0)
        u_sh = jnp.where(keep, pltpu.roll(u, k, axis=1), 0.0)
        u = a * u_sh + u
        a = a * a_sh
        k *= 2
    h = h0
    groups = []
    for g in range(G):
        blk = u[g] + a[g] * h
        groups.append(blk)
        h = blk[SUBLANES - 1:SUBLANES, :]
    return jnp.concatenate(groups, axis=0)


def _rglru_kernel(xp_ref, x_ref, gr_ref, cw_ref, cb_ref, wax_ref, ba_ref, bx_ref, lam_ref, ng_ref,
                  y_ref, h_ref):
    t = pl.program_id(1)
    T = x_ref.shape[1]
    RW = x_ref.shape[2]
    bd = RW // R_BLOCKS

    @pl.when(t == 0)
    def _():
        h_ref[...] = jnp.zeros_like(h_ref)

    prev = jnp.where(t > 0, xp_ref[0], 0.0)
    xr = _causal_conv(x_ref[0], prev, cw_ref[...]) + cb_ref[...]
    xrb = xr.astype(BF16)
    ls = _log_sigmoid(lam_ref[...])
    for n in range(R_BLOCKS):
        sl = slice(n * bd, (n + 1) * bd)
        z = jnp.dot(xrb[:, sl], wax_ref[n], preferred_element_type=F32)
        r_gate = jax.nn.sigmoid(z[:, :bd] + ba_ref[:, sl])
        i_gate = jax.nn.sigmoid(z[:, bd:] + bx_ref[:, sl])
        log_a = LRU_C * r_gate * ls[:, sl]
        a = jnp.exp(log_a)
        u = jnp.sqrt(-jnp.tanh(log_a) * (a * a + 1.0)) * (i_gate * xr[:, sl])
        hseq = _lru_scan(a, u, h_ref[:, sl])
        h_ref[:, sl] = hseq[T - 1:T, :]
        y = hseq * jax.nn.gelu(gr_ref[0, :, sl])
        y = y * lax.rsqrt(jnp.mean(y * y, axis=-1, keepdims=True) + EPS) * ng_ref[:, sl]
        y_ref[0, :, sl] = y.astype(BF16)


def _rglru(proj3, lru_conv_w, lru_conv_b, wax, b_a, b_x, lam, ng, T, xr_blk, gr_blk):
    B, S, _ = proj3.shape
    RW = lam.shape[-1]
    bd = RW // R_BLOCKS
    t8 = T // SUBLANES
    vec = pl.BlockSpec((1, RW), lambda b, t: (0, 0))
    return pl.pallas_call(
        _rglru_kernel,
        grid=(B, S // T),
        in_specs=[pl.BlockSpec((1, SUBLANES, RW), lambda b, t: (b, jnp.maximum(t * t8 - 1, 0), xr_blk)),
                  pl.BlockSpec((1, T, RW), lambda b, t: (b, t, xr_blk)),
                  pl.BlockSpec((1, T, RW), lambda b, t: (b, t, gr_blk)),
                  pl.BlockSpec((CONV_WIDTH, RW), lambda b, t: (0, 0)),
                  vec,
                  pl.BlockSpec((R_BLOCKS, bd, 2 * bd), lambda b, t: (0, 0, 0)),
                  vec, vec, vec, vec],
        out_specs=pl.BlockSpec((1, T, RW), lambda b, t: (b, t, 0)),
        out_shape=jax.ShapeDtypeStruct((B, S, RW), BF16),
        scratch_shapes=[pltpu.VMEM((1, RW), F32)],
        compiler_params=_cparams(("arbitrary", "arbitrary")),
        name="rglru",
    )(proj3, proj3, proj3, lru_conv_w, lru_conv_b, wax, b_a, b_x, lam, ng)


def _rows_to_token_tiles(x, stage_ref, tiles_ref):
    T, D = x.shape
    nq = D // LANES
    for q in range(nq):
        stage_ref[pl.ds(q, T, stride=nq), :] = x[:, q * LANES:(q + 1) * LANES]

    def per_token(t, carry):
        tiles_ref[t] = stage_ref[pl.ds(pl.multiple_of(t * nq, nq), nq), :].astype(BF16)
        return carry

    lax.fori_loop(0, T, per_token, 0, unroll=8)


def _token_tiles_to_rows(tiles_ref, stage_ref, dtype):
    T, nq, _ = tiles_ref.shape

    def per_token(t, carry):
        stage_ref[pl.ds(pl.multiple_of(t * nq, nq), nq), :] = tiles_ref[t].astype(F32)
        return carry

    lax.fori_loop(0, T, per_token, 0, unroll=8)
    return jnp.concatenate([stage_ref[pl.ds(q, T, stride=nq), :].astype(dtype) for q in range(nq)],
                           axis=1)


def _outproj_kernel(x_ref, ym_ref, yr_ref, mod_ref, wm_ref, wr_ref, wrt_ref, brt_ref,
                    x1_ref, hp_ref, lt_ref, stage_ref):
    mix = (jnp.dot(ym_ref[...], wm_ref[...], preferred_element_type=F32)
           + jnp.dot(yr_ref[...], wr_ref[...], preferred_element_type=F32))
    x1 = x_ref[...] + mod_ref[0, 2:3, :] * mix
    x1_ref[...] = x1
    ms = jnp.mean(x1 * x1, axis=-1, keepdims=True)
    hn = x1 * lax.rsqrt(ms + EPS) * (1.0 + mod_ref[0, 4:5, :]) + mod_ref[0, 3:4, :]
    lt_ref[...] = lax.dot_general(wrt_ref[...], hn.astype(BF16), _NT,
                                  preferred_element_type=F32) + brt_ref[:, 0:1]
    _rows_to_token_tiles(hn, stage_ref, hp_ref)


def _out_proj(x2, ym2, yr2, mod3, w_out_b, w_rt_t, b_rt, S, tm):
    N, D = x2.shape
    MW = ym2.shape[1]
    RW = yr2.shape[1]
    assert MW == RW, "the two head groups share one row-block size of w_out"
    NR = w_rt_t.shape[0]
    spb = S // tm
    return pl.pallas_call(
        _outproj_kernel,
        grid=(N // tm,),
        in_specs=[pl.BlockSpec((tm, D), lambda i: (i, 0)),
                  pl.BlockSpec((tm, MW), lambda i: (i, 0)),
                  pl.BlockSpec((tm, RW), lambda i: (i, 0)),
                  pl.BlockSpec((1, 6, D), lambda i: (i // spb, 0, 0)),
                  pl.BlockSpec((MW, D), lambda i: (0, 0)),
                  pl.BlockSpec((RW, D), lambda i: (1, 0)),
                  pl.BlockSpec((NR, D), lambda i: (0, 0)),
                  pl.BlockSpec((NR, LANES), lambda i: (0, 0))],
        out_specs=[pl.BlockSpec((tm, D), lambda i: (i, 0)),
                   pl.BlockSpec((tm, D // LANES, LANES), lambda i: (i, 0, 0)),
                   pl.BlockSpec((NR, tm), lambda i: (0, i))],
        out_shape=[jax.ShapeDtypeStruct((N, D), F32),
                   jax.ShapeDtypeStruct((N, D // LANES, LANES), BF16),
                   jax.ShapeDtypeStruct((NR, N), F32)],
        scratch_shapes=[pltpu.VMEM((tm * (D // LANES), LANES), F32)],
        compiler_params=_cparams(("arbitrary",)),
        name="out_proj",
    )(x2, ym2, yr2, mod3, w_out_b, w_out_b, w_rt_t, b_rt)


def _router_kernel(lt_ref, ri_ref, rw_ref, meta_ref, pcol_ref, carry_ref):
    i = pl.program_id(0)
    tm = lt_ref.shape[1]
    E8 = EXPERTS_PER_GROUP

    @pl.when(i == 0)
    def _():
        carry_ref[...] = jnp.zeros_like(carry_ref)
        meta_ref[...] = jnp.zeros_like(meta_ref)
        pcol_ref[...] = jnp.zeros_like(pcol_ref)

    sub = lax.broadcasted_iota(I32, (SUBLANES, tm), 0)
    gl = jnp.where(sub < N_GROUPS, lt_ref[N_EXPERTS:N_EXPERTS + SUBLANES, :], -jnp.inf)
    ge = jnp.exp(gl - jnp.max(gl, axis=0, keepdims=True))
    pg = ge / jnp.sum(ge, axis=0, keepdims=True)
    pg_sel = jnp.max(pg, axis=0, keepdims=True)
    g_sel = jnp.min(jnp.where(pg == pg_sel, sub, SUBLANES), axis=0, keepdims=True)

    el = lt_ref[(N_GROUPS - 1) * E8:N_GROUPS * E8, :]
    for g in range(N_GROUPS - 2, -1, -1):
        el = jnp.where(g_sel == g, lt_ref[g * E8:(g + 1) * E8, :], el)
    ee = jnp.exp(el - jnp.max(el, axis=0, keepdims=True))
    pe = ee / jnp.sum(ee, axis=0, keepdims=True)
    p0 = jnp.max(pe, axis=0, keepdims=True)
    i0 = jnp.min(jnp.where(pe == p0, sub, SUBLANES), axis=0, keepdims=True)
    pe1 = jnp.where(sub == i0, -1.0, pe)
    p1 = jnp.max(pe1, axis=0, keepdims=True)
    i1 = jnp.min(jnp.where(pe1 == p1, sub, SUBLANES), axis=0, keepdims=True)
    psum = p0 + p1
    w0 = pg_sel * p0 / psum
    w1 = pg_sel * p1 / psum
    e0 = g_sel * E8 + i0
    e1 = g_sel * E8 + i1

    eid = lax.broadcasted_iota(I32, (N_EXPERTS, tm), 0)
    oh0 = (eid == e0).astype(F32)
    oh1 = (eid == e1).astype(F32)
    oh = oh0 + oh1
    r_ = lax.broadcasted_iota(I32, (tm, tm), 0)
    c_ = lax.broadcasted_iota(I32, (tm, tm), 1)
    before = (r_ < c_).astype(BF16)
    cnt = carry_ref[:, 0:1] + jnp.dot(oh.astype(BF16), before, preferred_element_type=F32)
    pos0 = jnp.sum(oh0 * cnt, axis=0, keepdims=True)
    pos1 = jnp.sum(oh1 * cnt, axis=0, keepdims=True)
    new_carry = carry_ref[...] + jnp.sum(oh, axis=1, keepdims=True)
    carry_ref[...] = new_carry

    ri = jnp.where(sub == 0, e0, 0)
    ri = jnp.where(sub == 1, e1, ri)
    ri = jnp.where(sub == 2, pos0.astype(I32), ri)
    ri = jnp.where(sub == 3, pos1.astype(I32), ri)
    ri_ref[...] = ri
    rw_ref[...] = jnp.where(sub == 0, w0, jnp.where(sub == 1, w1, 0.0))

    @pl.when(i == pl.num_programs(0) - 1)
    def _():
        nbp = meta_ref.shape[1]
        cnt_col = new_carry[:, 0:1]
        padded_col = jnp.ceil(cnt_col / MOE_ROWS) * MOE_ROWS
        nonempty = cnt_col > 0.0
        e_sub = lax.broadcasted_iota(I32, (N_EXPERTS, nbp), 0)
        lane = lax.broadcasted_iota(I32, (N_EXPERTS, nbp), 1)
        padded_row = jnp.sum(jnp.where(e_sub == lane, padded_col, 0.0), axis=0, keepdims=True)
        pend_row = jnp.sum(jnp.where(e_sub <= lane, padded_col, 0.0), axis=0, keepdims=True)
        pend_col = jnp.sum(jnp.where(lane <= e_sub, padded_row, 0.0), axis=1, keepdims=True)
        blk_start = lane.astype(F32) * MOE_ROWS
        e_f = e_sub.astype(F32)
        be = jnp.sum(jnp.where(pend_col <= blk_start, 1.0, 0.0), axis=0, keepdims=True)
        be = jnp.minimum(be, N_EXPERTS - 1.0)
        nxt = jnp.min(jnp.where(jnp.logical_and(e_f > be, nonempty), e_f, float(N_EXPERTS)),
                      axis=0, keepdims=True)
        run = jnp.sum(jnp.where(jnp.logical_and(e_f < be, nonempty), 1.0, 0.0),
                      axis=0, keepdims=True)
        cnt_row = jnp.sum(jnp.where(e_sub == lane, cnt_col, 0.0), axis=0, keepdims=True)
        n_used = pend_row[:, N_EXPERTS - 1:N_EXPERTS] / MOE_ROWS
        sub8 = lax.broadcasted_iota(I32, (SUBLANES, nbp), 0)
        meta = jnp.where(sub8 == META_BLOCK_E, be, 0.0)
        meta = jnp.where(sub8 == META_NEXT_E, nxt, meta)
        meta = jnp.where(sub8 == META_SLOT, run - 2.0 * jnp.floor(run * 0.5), meta)
        meta = jnp.where(sub8 == META_NUSED, n_used, meta)
        meta = jnp.where(sub8 == META_PAD_LO, pend_row - padded_row + cnt_row, meta)
        meta = jnp.where(sub8 == META_PAD_N, padded_row - cnt_row, meta)
        meta_ref[...] = meta.astype(I32)
        pcol_ref[...] = jnp.broadcast_to(pend_col - padded_col, pcol_ref.shape)


def _router(logits_t, tm, n_blocks):
    NR, N = logits_t.shape
    nbp = max(-(-n_blocks // LANES), 1) * LANES
    return pl.pallas_call(
        _router_kernel,
        grid=(N // tm,),
        in_specs=[pl.BlockSpec((NR, tm), lambda i: (0, i))],
        out_specs=[pl.BlockSpec((SUBLANES, tm), lambda i: (0, i)),
                   pl.BlockSpec((SUBLANES, tm), lambda i: (0, i)),
                   pl.BlockSpec((SUBLANES, nbp), lambda i: (0, 0)),
                   pl.BlockSpec((N_EXPERTS, LANES), lambda i: (0, 0))],
        out_shape=[jax.ShapeDtypeStruct((SUBLANES, N), I32),
                   jax.ShapeDtypeStruct((SUBLANES, N), F32),
                   jax.ShapeDtypeStruct((SUBLANES, nbp), I32),
                   jax.ShapeDtypeStruct((N_EXPERTS, LANES), F32)],
        scratch_shapes=[pltpu.VMEM((N_EXPERTS, LANES), F32)],
        compiler_params=_cparams(("arbitrary",)),
        name="router",
    )(logits_t)


def _slots_kernel(ri_ref, pcol_ref, d_ref):
    tm = ri_ref.shape[1]
    eid = lax.broadcasted_iota(I32, (N_EXPERTS, tm), 0)
    pstart = pcol_ref[:, 0:1]
    sub = lax.broadcasted_iota(I32, (SUBLANES, tm), 0)
    out = jnp.zeros((SUBLANES, tm), I32)
    for kk in range(TOP_K):
        first = jnp.sum(jnp.where(eid == ri_ref[kk:kk + 1, :], pstart, 0.0), axis=0, keepdims=True)
        out = jnp.where(sub == kk, first.astype(I32) + ri_ref[TOP_K + kk:TOP_K + kk + 1, :], out)
    d_ref[...] = out


def _slots(ri, pcol, tm):
    _, N = ri.shape
    return pl.pallas_call(
        _slots_kernel,
        grid=(N // tm,),
        in_specs=[pl.BlockSpec((SUBLANES, tm), lambda i: (0, i)),
                  pl.BlockSpec((N_EXPERTS, LANES), lambda i: (0, 0))],
        out_specs=pl.BlockSpec((SUBLANES, tm), lambda i: (0, i)),
        out_shape=jax.ShapeDtypeStruct((SUBLANES, N), I32),
        compiler_params=_cparams(("arbitrary",)),
        name="slots",
    )(ri, pcol)


ROW_DMA_UNROLL = 8


def _row_copy(src, dst, sem):
    return pltpu.make_async_copy(src, dst, sem)


def _unrolled(n, fn):
    def trip(g, carry):
        for u in range(ROW_DMA_UNROLL):
            fn(g * ROW_DMA_UNROLL + u)
        return carry
    lax.fori_loop(0, n // ROW_DMA_UNROLL, trip, 0)


def _dispatch_kernel(meta_ref, dest_ref, hp_ref, xs_ref, zero_ref, sem):
    i = pl.program_id(0)
    tm = hp_ref.shape[0]
    n_blocks = xs_ref.shape[0] // MOE_ROWS

    def issue(t):
        for kk in range(TOP_K):
            _row_copy(hp_ref.at[t], xs_ref.at[dest_ref[kk, t]], sem.at[0]).start(priority=kk)

    _unrolled(tm, issue)
    _unrolled(TOP_K * tm, lambda t: _row_copy(hp_ref.at[0], xs_ref.at[0], sem.at[0]).wait())

    @pl.when(i == pl.num_programs(0) - 1)
    def _():
        zero_ref[...] = jnp.zeros_like(zero_ref)
        for e in range(N_EXPERTS):
            lo = meta_ref[META_PAD_LO, e]
            n_pad = meta_ref[META_PAD_N, e]

            def fill(r, carry):
                _row_copy(zero_ref.at[0], xs_ref.at[lo + r], sem.at[1]).start()
                return carry

            def drain(r, carry):
                _row_copy(zero_ref.at[0], xs_ref.at[0], sem.at[1]).wait()
                return carry

            lax.fori_loop(0, n_pad, fill, 0)
            lax.fori_loop(0, n_pad, drain, 0)

        def block_copy(b):
            return _row_copy(zero_ref, xs_ref.at[pl.ds(b * MOE_ROWS, MOE_ROWS)], sem.at[1])

        n_used = meta_ref[META_NUSED, 0]
        lax.fori_loop(n_used, n_blocks, lambda b, c: (block_copy(b).start(), c)[1], 0)
        lax.fori_loop(n_used, n_blocks, lambda b, c: (block_copy(0).wait(), c)[1], 0)


def _dispatch(meta, dest, hp, n_slots, tm):
    N, nq, _ = hp.shape
    grid_spec = pltpu.PrefetchScalarGridSpec(
        num_scalar_prefetch=1,
        grid=(N // tm,),
        in_specs=[pl.BlockSpec((SUBLANES, tm), lambda i, m: (0, i), memory_space=pltpu.SMEM),
                  pl.BlockSpec((tm, nq, LANES), lambda i, m: (i, 0, 0))],
        out_specs=pl.BlockSpec(memory_space=pl.ANY),
        scratch_shapes=[pltpu.VMEM((MOE_ROWS, nq, LANES), BF16), pltpu.SemaphoreType.DMA((2,))],
    )
    return pl.pallas_call(
        _dispatch_kernel,
        grid_spec=grid_spec,
        out_shape=jax.ShapeDtypeStruct((n_slots, nq, LANES), BF16),
        compiler_params=_cparams(("arbitrary",)),
        name="dispatch",
    )(meta, dest, hp)


def _expert_kernel(meta_ref, xs_ref, wg_hbm, wu_hbm, wd_hbm, ys_ref, wbuf_g, wbuf_u, wbuf_d,
                   wgu_s, wd_s, stage_ref, sem):
    i = pl.program_id(0)
    DE = wbuf_g.shape[2]
    e = meta_ref[META_BLOCK_E, i]
    e_prev = meta_ref[META_BLOCK_E, jnp.maximum(i - 1, 0)]
    used = i < meta_ref[META_NUSED, 0]
    first = jnp.logical_and(used, jnp.logical_or(i == 0, e != e_prev))
    slot = meta_ref[META_SLOT, i]
    nxt = meta_ref[META_NEXT_E, i]

    def weight_copies(ex, s):
        return (_row_copy(wg_hbm.at[ex], wbuf_g.at[s], sem.at[s]),
                _row_copy(wu_hbm.at[ex], wbuf_u.at[s], sem.at[s]),
                _row_copy(wd_hbm.at[ex], wbuf_d.at[s], sem.at[s]))

    @pl.when(jnp.logical_and(used, i == 0))
    def _():
        for cp in weight_copies(e, slot):
            cp.start()

    @pl.when(first)
    def _():
        @pl.when(nxt < N_EXPERTS)
        def _():
            for cp in weight_copies(nxt, 1 - slot):
                cp.start()

        for cp in weight_copies(e, slot):
            cp.wait()
        wgu_s[:, :DE] = wbuf_g[slot].astype(BF16)
        wgu_s[:, DE:] = wbuf_u[slot].astype(BF16)
        wd_s[...] = wbuf_d[slot].astype(BF16)

    @pl.when(used)
    def _():
        xb = _token_tiles_to_rows(xs_ref, stage_ref, BF16)
        gu = jnp.dot(xb, wgu_s[...], preferred_element_type=F32)
        hb = _silu(gu[:, :DE]) * gu[:, DE:]
        y = jnp.dot(hb.astype(BF16), wd_s[...], preferred_element_type=F32)
        _rows_to_token_tiles(y, stage_ref, ys_ref)

    @pl.when(jnp.logical_not(used))
    def _():
        ys_ref[...] = jnp.zeros_like(ys_ref)


def _experts(meta, xs, w_e_gate, w_e_up, w_e_down):
    n_slots, nq, _ = xs.shape
    _, D, DE = w_e_gate.shape
    n_blocks = n_slots // MOE_ROWS
    hbm = pl.BlockSpec(memory_space=pl.ANY)
    grid_spec = pltpu.PrefetchScalarGridSpec(
        num_scalar_prefetch=1,
        grid=(n_blocks,),
        in_specs=[pl.BlockSpec((MOE_ROWS, nq, LANES), lambda i, m: (i, 0, 0)), hbm, hbm, hbm],
        out_specs=pl.BlockSpec((MOE_ROWS, nq, LANES), lambda i, m: (i, 0, 0)),
        scratch_shapes=[pltpu.VMEM((2, D, DE), F32), pltpu.VMEM((2, D, DE), F32),
                        pltpu.VMEM((2, DE, D), F32),
                        pltpu.VMEM((D, 2 * DE), BF16), pltpu.VMEM((DE, D), BF16),
                        pltpu.VMEM((MOE_ROWS * nq, LANES), F32),
                        pltpu.SemaphoreType.DMA((2,))],
    )
    return pl.pallas_call(
        _expert_kernel,
        grid_spec=grid_spec,
        out_shape=jax.ShapeDtypeStruct((n_slots, nq, LANES), BF16),
        compiler_params=_cparams(("arbitrary",)),
        name="experts",
    )(meta, xs, w_e_gate, w_e_up, w_e_down)


def _combine_kernel(rcur_ref, rnext_ref, x1_ref, wt_ref, mod_ref, fg_ref, ys_ref, o_ref, gbuf,
                    stage_ref, sem):
    i = pl.program_id(0)
    n = pl.num_programs(0)
    tm = x1_ref.shape[0]
    slot = i % 2

    def gather(dest_ref, s):
        def issue(t):
            for kk in range(TOP_K):
                _row_copy(ys_ref.at[dest_ref[kk, t]], gbuf.at[s, kk, t],
                          sem.at[s]).start(priority=kk)
        _unrolled(tm, issue)

    @pl.when(i == 0)
    def _():
        gather(rcur_ref, 0)

    @pl.when(i + 1 < n)
    def _():
        gather(rnext_ref, 1 - slot)

    _unrolled(TOP_K * tm,
              lambda t: _row_copy(ys_ref.at[0], gbuf.at[slot, 0, 0], sem.at[slot]).wait())

    wt = wt_ref[...]
    y = wt[:, 0:1] * _token_tiles_to_rows(gbuf.at[slot, 0], stage_ref, F32)
    y = y + wt[:, 1:2] * _token_tiles_to_rows(gbuf.at[slot, 1], stage_ref, F32)
    x2 = x1_ref[...] + mod_ref[0, 5:6, :] * y
    ms = jnp.mean(x2 * x2, axis=-1, keepdims=True)
    o_ref[...] = x2 * lax.rsqrt(ms + EPS) * fg_ref[...]


def _combine(dest, x1, wts, mod3, final_g, ys, S, tm):
    N, D = x1.shape
    spb = S // tm
    nsteps = N // tm
    return pl.pallas_call(
        _combine_kernel,
        grid=(nsteps,),
        in_specs=[pl.BlockSpec((SUBLANES, tm), lambda i: (0, i), memory_space=pltpu.SMEM),
                  pl.BlockSpec((SUBLANES, tm), lambda i: (0, jnp.minimum(i + 1, nsteps - 1)),
                               memory_space=pltpu.SMEM),
                  pl.BlockSpec((tm, D), lambda i: (i, 0)),
                  pl.BlockSpec((tm, SUBLANES), lambda i: (i, 0)),
                  pl.BlockSpec((1, 6, D), lambda i: (i // spb, 0, 0)),
                  pl.BlockSpec((1, D), lambda i: (0, 0)),
                  pl.BlockSpec(memory_space=pl.ANY)],
        out_specs=pl.BlockSpec((tm, D), lambda i: (i, 0)),
        out_shape=jax.ShapeDtypeStruct((N, D), F32),
        scratch_shapes=[pltpu.VMEM((2, TOP_K, tm, D // LANES, LANES), BF16),
                        pltpu.VMEM((tm * (D // LANES), LANES), F32),
                        pltpu.SemaphoreType.DMA((2,))],
        compiler_params=_cparams(("arbitrary",)),
        name="combine",
    )(dest, dest, x1, wts, mod3, final_g.reshape(1, D), ys)


def _tiles(S):
    def fit(t):
        return min(t, S)
    return dict(prep_tr=256, inproj_tm=fit(2048), inproj_tn=256, chunk=fit(256), lru_t=fit(256),
                outproj_tm=fit(512), router_tm=fit(512), dispatch_tm=fit(256), combine_tm=fit(256))


def _layer(x, mod, w_in, b_gates, conv_qk, mh_norm_g, lru_conv_w, lru_conv_b, w_lru_a, b_lru_a,
           w_lru_x, b_lru_x, lru_lambda, lru_norm_g, w_out, w_group, b_group, w_router, b_router,
           w_e_gate, w_e_up, w_e_down, final_g):
    B, S, D = x.shape
    N = B * S
    MW = mh_norm_g.shape[-1]
    RW = lru_lambda.shape[-1]
    H = M_HEADS
    tl = _tiles(S)
    x2 = x.reshape(N, D)
    mod3 = mod.reshape(B, 6, D)

    n_qkvo = 4 * MW
    w_in_t = w_in.T
    w_main_t = _prep_w_in(w_in_t, n_qkvo, tl["prep_tr"])
    w_gate_t = w_in_t[n_qkvo:n_qkvo + 2 * H]
    w_ift = jnp.zeros((SUBLANES, D), F32).at[:2 * H].set(w_gate_t).astype(BF16)
    w_if_t = jnp.zeros((LANES, D), F32).at[:2 * H].set(w_gate_t).astype(BF16)
    proj, gates, gates_t = _in_proj(x2, mod3, w_main_t, w_if_t, w_ift, S, tl["inproj_tm"],
                                    tl["inproj_tn"])
    proj3 = proj.reshape(B, S, -1)

    bg_row = jnp.zeros((1, LANES), F32).at[0, :2 * H].set(b_gates)
    bg_col = jnp.zeros((SUBLANES, LANES), F32).at[:2 * H, :].set(b_gates[:, None])
    ym = _mlstm(proj3, gates.reshape(B, S, LANES), gates_t, bg_row, bg_col, conv_qk,
                mh_norm_g.reshape(1, MW), tl["chunk"])

    wax = jnp.concatenate([w_lru_a, w_lru_x], axis=-1).astype(BF16)
    yr = _rglru(proj3, lru_conv_w, lru_conv_b.reshape(1, RW), wax, b_lru_a.reshape(1, RW),
                b_lru_x.reshape(1, RW), lru_lambda.reshape(1, RW), lru_norm_g.reshape(1, RW),
                tl["lru_t"], n_qkvo // RW, n_qkvo // RW + 1)

    NR = N_EXPERTS + SUBLANES
    w_rt_t = (jnp.zeros((NR, D), F32).at[:N_EXPERTS].set(w_router.T)
              .at[N_EXPERTS:N_EXPERTS + N_GROUPS].set(w_group.T)).astype(BF16)
    b_rt = (jnp.zeros((NR, LANES), F32).at[:N_EXPERTS, :].set(b_router[:, None])
            .at[N_EXPERTS:N_EXPERTS + N_GROUPS, :].set(b_group[:, None]))
    x1, hp, logits_t = _out_proj(x2, ym.reshape(N, MW), yr.reshape(N, RW), mod3,
                                 w_out.astype(BF16), w_rt_t, b_rt, S, tl["outproj_tm"])

    n_slots = N * TOP_K + N_EXPERTS * MOE_ROWS
    ri, rw, meta, pcol = _router(logits_t, tl["router_tm"], n_slots // MOE_ROWS)
    dest = _slots(ri, pcol, tl["router_tm"])
    xs = _dispatch(meta, dest, hp, n_slots, tl["dispatch_tm"])
    ys = _experts(meta, xs, w_e_gate, w_e_up, w_e_down)
    out = _combine(dest, x1, rw.T, mod3, final_g, ys, S, tl["combine_tm"])
    return out.reshape(B, S, D)


def kernel(x, c, w_ada, b_ada, w_in, b_gates, conv_qk, mh_norm_g, lru_conv_w, lru_conv_b, w_lru_a, b_lru_a, w_lru_x, b_lru_x, lru_lambda, lru_norm_g, w_out, w_group, b_group, w_router, b_router, w_e_gate, w_e_up, w_e_down, final_g):
    depth = w_ada.shape[0]
    assert depth == 1, "single trunk layer"
    l = 0
    mod = _ada_mod(c, w_ada[l], b_ada[l])
    return _layer(x, mod, w_in[l], b_gates[l], conv_qk[l], mh_norm_g[l], lru_conv_w[l],
                  lru_conv_b[l], w_lru_a[l], b_lru_a[l], w_lru_x[l], b_lru_x[l], lru_lambda[l],
                  lru_norm_g[l], w_out[l], w_group[l], b_group[l], w_router[l], b_router[l],
                  w_e_gate[l], w_e_up[l], w_e_down[l], final_g)
```

```python
import functools

import jax
import jax.numpy as jnp
from jax import lax
from jax.experimental import pallas as pl
from jax.experimental.pallas import tpu as pltpu

F32 = jnp.float32
BF16 = jnp.bfloat16
U32 = jnp.uint32
I32 = jnp.int32

EPS = 1e-6
M_HEADS = 4
R_BLOCKS = 8
CONV_WIDTH = 4
LRU_C = 8.0
N_GROUPS = 4
EXPERTS_PER_GROUP = 8
N_EXPERTS = N_GROUPS * EXPERTS_PER_GROUP
TOP_K = 2

LANES = 128
SUBLANES = 8
VMEM_LIMIT = 56 * 1024 * 1024
PROLOGUE_ROWS = 256

MOE_ROWS = 256
META_BLOCK_E, META_NEXT_E, META_SLOT, META_NUSED, META_PAD_LO, META_PAD_N = range(6)


def _cparams(sem, vmem=VMEM_LIMIT):
    return pltpu.CompilerParams(dimension_semantics=sem, vmem_limit_bytes=vmem)


def _log_sigmoid(x):
    return jnp.minimum(x, 0.0) - jnp.log1p(jnp.exp(-jnp.abs(x)))


def _silu(x):
    return x * jax.nn.sigmoid(x)


def _split3(x):
    hi = x.astype(BF16)
    r1 = x - hi.astype(F32)
    mid = r1.astype(BF16)
    lo = (r1 - mid.astype(F32)).astype(BF16)
    return hi, mid, lo


def _causal_conv(cur, prev8, w4):
    T, C = cur.shape
    G = T // SUBLANES
    x3 = cur.reshape(G, SUBLANES, C)
    p3 = prev8.reshape(1, SUBLANES, C)
    rid = lax.broadcasted_iota(I32, x3.shape, 1)
    acc = x3 * w4[CONV_WIDTH - 1:CONV_WIDTH, :]
    for d in range(1, CONV_WIDTH):
        rot = pltpu.roll(x3, d, axis=1)
        before = jnp.concatenate([pltpu.roll(p3, d, axis=1), rot[:G - 1]], axis=0)
        acc = acc + jnp.where(rid < d, before, rot) * w4[CONV_WIDTH - 1 - d:CONV_WIDTH - d, :]
    return acc.reshape(T, C)


def _ada_kernel(c_ref, w_ref, b_ref, o_ref):
    s = _silu(c_ref[...])
    o_ref[...] = jnp.dot(s.astype(BF16), w_ref[...].astype(BF16),
                         preferred_element_type=F32) + b_ref[...]


def _ada_mod(c, w_ada, b_ada):
    B, D = c.shape
    n6 = w_ada.shape[1]
    tn = 1024
    cp = jnp.zeros((SUBLANES, D), F32).at[:B].set(c)
    out = pl.pallas_call(
        _ada_kernel,
        grid=(n6 // tn,),
        in_specs=[pl.BlockSpec((SUBLANES, D), lambda j: (0, 0)),
                  pl.BlockSpec((D, tn), lambda j: (0, j)),
                  pl.BlockSpec((1, tn), lambda j: (0, j))],
        out_specs=pl.BlockSpec((SUBLANES, tn), lambda j: (0, j)),
        out_shape=jax.ShapeDtypeStruct((SUBLANES, n6), F32),
        compiler_params=_cparams(("arbitrary",)),
        name="ada_mod",
    )(cp, w_ada, b_ada.reshape(1, n6))
    return out[:B]


_NT = (((1,), (1,)), ((), ()))


def _prep_kernel(w_ref, wm_ref):
    wm_ref[...] = w_ref[...].astype(BF16)


def _prep_w_in(w_in_t, n_qkvo, tr):
    nin, D = w_in_t.shape
    n_gate = 2 * M_HEADS
    nm = nin - n_gate
    assert n_qkvo % tr == 0 and nm % tr == 0 and n_gate % SUBLANES == 0
    return pl.pallas_call(
        _prep_kernel,
        grid=(nm // tr,),
        in_specs=[pl.BlockSpec((pl.Element(tr), pl.Element(D)),
                               lambda i: (pl.multiple_of(
                                   i * tr + jnp.where(i * tr < n_qkvo, 0, n_gate), SUBLANES), 0))],
        out_specs=pl.BlockSpec((tr, D), lambda i: (i, 0)),
        out_shape=jax.ShapeDtypeStruct((nm, D), BF16),
        compiler_params=_cparams(("arbitrary",)),
        name="prep_w_in",
    )(w_in_t)


def _inproj_kernel(x_ref, mod_ref, w_ref, wif_ref, wift_ref, o_ref, g_ref, gt_ref, hn_ref):
    j = pl.program_id(1)

    @pl.when(j == 0)
    def _():
        tm = x_ref.shape[0]
        rc = min(tm, PROLOGUE_ROWS)
        for r in range(tm // rc):
            rows = slice(r * rc, (r + 1) * rc)
            x = x_ref[rows, :]
            ms = jnp.mean(x * x, axis=-1, keepdims=True)
            hn = x * lax.rsqrt(ms + EPS) * (1.0 + mod_ref[0, 1:2, :]) + mod_ref[0, 0:1, :]
            hb = hn.astype(BF16)
            hn_ref[rows, :] = hb
            g_ref[rows, :] = lax.dot_general(hb, wif_ref[...], _NT, preferred_element_type=F32)
            gt_ref[:, rows] = lax.dot_general(wift_ref[...], hb, _NT, preferred_element_type=F32)

    o_ref[...] = lax.dot_general(hn_ref[...], w_ref[...], _NT, preferred_element_type=F32)


def _in_proj(x2, mod3, w_main_t, w_if_t, w_ift, S, tm, tn):
    N, D = x2.shape
    nw = w_main_t.shape[0]
    spb = S // tm
    return pl.pallas_call(
        _inproj_kernel,
        grid=(N // tm, nw // tn),
        in_specs=[pl.BlockSpec((tm, D), lambda i, j: (i, 0)),
                  pl.BlockSpec((1, 6, D), lambda i, j: (i // spb, 0, 0)),
                  pl.BlockSpec((tn, D), lambda i, j: (j, 0)),
                  pl.BlockSpec((LANES, D), lambda i, j: (0, 0)),
                  pl.BlockSpec((SUBLANES, D), lambda i, j: (0, 0))],
        out_specs=[pl.BlockSpec((tm, tn), lambda i, j: (i, j)),
                   pl.BlockSpec((tm, LANES), lambda i, j: (i, 0)),
                   pl.BlockSpec((SUBLANES, tm), lambda i, j: (0, i))],
        out_shape=[jax.ShapeDtypeStruct((N, nw), F32),
                   jax.ShapeDtypeStruct((N, LANES), F32),
                   jax.ShapeDtypeStruct((SUBLANES, N), F32)],
        scratch_shapes=[pltpu.VMEM((tm, D), BF16)],
        compiler_params=_cparams(("arbitrary", "arbitrary")),
        name="in_proj",
    )(x2, mod3, w_main_t, w_if_t, w_ift)


def _mlstm_kernel(qp_ref, kp_ref, q_ref, k_ref, v_ref, o_ref, g_ref, gt_ref, bgr_ref, bgc_ref,
                  cw_ref, ng_ref, y_ref, C_ref, n_ref, m_ref):
    c = pl.program_id(1)
    L = q_ref.shape[1]
    MW = q_ref.shape[2]
    dh = MW // M_HEADS
    H = M_HEADS

    @pl.when(c == 0)
    def _():
        C_ref[...] = jnp.zeros_like(C_ref)
        n_ref[...] = jnp.zeros_like(n_ref)
        m_ref[...] = jnp.zeros_like(m_ref)

    row = lax.broadcasted_iota(I32, (L, L), 0)
    col = lax.broadcasted_iota(I32, (L, L), 1)
    causal = col <= row
    tri = causal.astype(BF16)
    tri_t = (row <= col).astype(BF16)

    gb = g_ref[0] + bgr_ref[...]
    gtb = gt_ref[...] + bgc_ref[:, 0:1]
    b_col_all = sum(jnp.dot(tri, p, preferred_element_type=F32) for p in _split3(_log_sigmoid(gb)))
    b_row_all = sum(jnp.dot(p, tri_t, preferred_element_type=F32) for p in _split3(_log_sigmoid(gtb)))

    has_prev = c > 0

    for h in range(H):
        hs = slice(h * dh, (h + 1) * dh)
        ig_col = gb[:, h:h + 1]
        b_col = b_col_all[:, H + h:H + h + 1]
        ig_row = gtb[h:h + 1, :]
        b_row = b_row_all[H + h:H + h + 1, :]
        b_last = b_row[:, L - 1:L]
        m_prev = m_ref[h][0:1, 0:1]

        qprev = jnp.where(has_prev, qp_ref[0, :, hs], 0.0)
        kprev = jnp.where(has_prev, kp_ref[0, :, hs], 0.0)
        q = _silu(_causal_conv(q_ref[0, :, hs], qprev, cw_ref[:, hs]))
        k = _silu(_causal_conv(k_ref[0, :, hs], kprev,
                               cw_ref[:, MW + h * dh:MW + (h + 1) * dh])) * (dh ** -0.5)
        qb = q.astype(BF16)
        kb = k.astype(BF16)
        vb = v_ref[0, :, hs].astype(BF16)

        qk = lax.dot_general(qb, kb, (((1,), (1,)), ((), ())), preferred_element_type=F32)
        dmat = jnp.where(causal, b_col - b_row + ig_row, -jnp.inf)
        inter = b_col + m_prev
        m_t = jnp.maximum(inter, jnp.max(dmat, axis=-1, keepdims=True))
        s = qk * jnp.exp(dmat - m_t)
        e_inter = jnp.exp(inter - m_t)
        C_old = C_ref[h]
        n_old = n_ref[h]
        num = (jnp.dot(s.astype(BF16), vb, preferred_element_type=F32)
               + e_inter * jnp.dot(qb, C_old.astype(BF16), preferred_element_type=F32))
        den = (jnp.sum(s, axis=-1, keepdims=True)
               + e_inter * jnp.sum(q * n_old, axis=-1, keepdims=True))
        hval = num / jnp.maximum(jnp.abs(den), jnp.exp(-m_t))

        g_col = b_last - b_col + ig_col
        g_row = b_last - b_row + ig_row
        m_new = jnp.maximum(b_last + m_prev, jnp.max(g_row, axis=-1, keepdims=True))
        wk = jnp.exp(g_col - m_new)
        decay = jnp.exp(b_last + m_prev - m_new)
        kw = k * wk
        C_ref[h] = decay * C_old + lax.dot_general(kw.astype(BF16), vb, (((0,), (0,)), ((), ())),
                                                   preferred_element_type=F32)
        n_ref[h] = decay * n_old + jnp.sum(kw, axis=0, keepdims=True)
        m_ref[h] = jnp.broadcast_to(m_new, m_ref.shape[1:])

        hnorm = hval * lax.rsqrt(jnp.mean(hval * hval, axis=-1, keepdims=True) + EPS)
        ym = hnorm * ng_ref[:, hs] * jax.nn.sigmoid(o_ref[0, :, hs])
        y_ref[0, :, hs] = ym.astype(BF16)


def _mlstm(proj3, gates3, gates_t, bg_row, bg_col, conv_qk, mh_norm_g, L):
    B, S, _ = proj3.shape
    MW = mh_norm_g.shape[-1]
    dh = MW // M_HEADS
    nc = S // L
    l8 = L // SUBLANES

    def prev_map(colblk):
        return lambda b, c: (b, jnp.maximum(c * l8 - 1, 0), colblk)

    def cur_map(colblk):
        return lambda b, c: (b, c, colblk)

    return pl.pallas_call(
        _mlstm_kernel,
        grid=(B, nc),
        in_specs=[pl.BlockSpec((1, SUBLANES, MW), prev_map(0)),
                  pl.BlockSpec((1, SUBLANES, MW), prev_map(1)),
                  pl.BlockSpec((1, L, MW), cur_map(0)),
                  pl.BlockSpec((1, L, MW), cur_map(1)),
                  pl.BlockSpec((1, L, MW), cur_map(2)),
                  pl.BlockSpec((1, L, MW), cur_map(3)),
                  pl.BlockSpec((1, L, LANES), lambda b, c: (b, c, 0)),
                  pl.BlockSpec((SUBLANES, L), lambda b, c: (0, b * nc + c)),
                  pl.BlockSpec((1, LANES), lambda b, c: (0, 0)),
                  pl.BlockSpec((SUBLANES, LANES), lambda b, c: (0, 0)),
                  pl.BlockSpec((CONV_WIDTH, 2 * MW), lambda b, c: (0, 0)),
                  pl.BlockSpec((1, MW), lambda b, c: (0, 0))],
        out_specs=pl.BlockSpec((1, L, MW), lambda b, c: (b, c, 0)),
        out_shape=jax.ShapeDtypeStruct((B, S, MW), BF16),
        scratch_shapes=[pltpu.VMEM((M_HEADS, dh, dh), F32),
                        pltpu.VMEM((M_HEADS, 1, dh), F32),
                        pltpu.VMEM((M_HEADS, SUBLANES, LANES), F32)],
        compiler_params=_cparams(("arbitrary", "arbitrary")),
        name="mlstm",
    )(proj3, proj3, proj3, proj3, proj3, proj3, gates3, gates_t, bg_row, bg_col, conv_qk,
      mh_norm_g)


def _lru_scan(a, u, h0):
    T, C = a.shape
    G = T // SUBLANES
    a = a.reshape(G, SUBLANES, C)
    u = u.reshape(G, SUBLANES, C)
    rid = lax.broadcasted_iota(I32, a.shape, 1)
    k = 1
    while k < SUBLANES:
        keep = rid >= k
        a_sh = jnp.where(keep, pltpu.roll(a, k, axis=1), 1.0)
        u_sh = jnp.where(keep, pltpu.roll(u, k, axis=1), 0.0)
        u = a * u_sh + u
        a = a * a_sh
        k *= 2
    h = h0
    groups = []
    for g in range(G):
        blk = u[g] + a[g] * h
        groups.append(blk)
        h = blk[SUBLANES - 1:SUBLANES, :]
    return jnp.concatenate(groups, axis=0)


def _rglru_kernel(xp_ref, x_ref, gr_ref, cw_ref, cb_ref, wax_ref, ba_ref, bx_ref, lam_ref, ng_ref,
                  y_ref, h_ref):
    t = pl.program_id(1)
    T = x_ref.shape[1]
    RW = x_ref.shape[2]
    bd = RW // R_BLOCKS

    @pl.when(t == 0)
    def _():
        h_ref[...] = jnp.zeros_like(h_ref)

    prev = jnp.where(t > 0, xp_ref[0], 0.0)
    xr = _causal_conv(x_ref[0], prev, cw_ref[...]) + cb_ref[...]
    xrb = xr.astype(BF16)
    ls = _log_sigmoid(lam_ref[...])
    for n in range(R_BLOCKS):
        sl = slice(n * bd, (n + 1) * bd)
        z = jnp.dot(xrb[:, sl], wax_ref[n], preferred_element_type=F32)
        r_gate = jax.nn.sigmoid(z[:, :bd] + ba_ref[:, sl])
        i_gate = jax.nn.sigmoid(z[:, bd:] + bx_ref[:, sl])
        log_a = LRU_C * r_gate * ls[:, sl]
        a = jnp.exp(log_a)
        u = jnp.sqrt(-jnp.tanh(log_a) * (a * a + 1.0)) * (i_gate * xr[:, sl])
        hseq = _lru_scan(a, u, h_ref[:, sl])
        h_ref[:, sl] = hseq[T - 1:T, :]
        y = hseq * jax.nn.gelu(gr_ref[0, :, sl])
        y = y * lax.rsqrt(jnp.mean(y * y, axis=-1, keepdims=True) + EPS) * ng_ref[:, sl]
        y_ref[0, :, sl] = y.astype(BF16)


def _rglru(proj3, lru_conv_w, lru_conv_b, wax, b_a, b_x, lam, ng, T, xr_blk, gr_blk):
    B, S, _ = proj3.shape
    RW = lam.shape[-1]
    bd = RW // R_BLOCKS
    t8 = T // SUBLANES
    vec = pl.BlockSpec((1, RW), lambda b, t: (0, 0))
    return pl.pallas_call(
        _rglru_kernel,
        grid=(B, S // T),
        in_specs=[pl.BlockSpec((1, SUBLANES, RW), lambda b, t: (b, jnp.maximum(t * t8 - 1, 0), xr_blk)),
                  pl.BlockSpec((1, T, RW), lambda b, t: (b, t, xr_blk)),
                  pl.BlockSpec((1, T, RW), lambda b, t: (b, t, gr_blk)),
                  pl.BlockSpec((CONV_WIDTH, RW), lambda b, t: (0, 0)),
                  vec,
                  pl.BlockSpec((R_BLOCKS, bd, 2 * bd), lambda b, t: (0, 0, 0)),
                  vec, vec, vec, vec],
        out_specs=pl.BlockSpec((1, T, RW), lambda b, t: (b, t, 0)),
        out_shape=jax.ShapeDtypeStruct((B, S, RW), BF16),
        scratch_shapes=[pltpu.VMEM((1, RW), F32)],
        compiler_params=_cparams(("arbitrary", "arbitrary")),
        name="rglru",
    )(proj3, proj3, proj3, lru_conv_w, lru_conv_b, wax, b_a, b_x, lam, ng)


def _rows_to_token_tiles(x, stage_ref, tiles_ref):
    T, D = x.shape
    nq = D // LANES
    for q in range(nq):
        stage_ref[pl.ds(q, T, stride=nq), :] = x[:, q * LANES:(q + 1) * LANES]

    def per_token(t, carry):
        tiles_ref[t] = stage_ref[pl.ds(pl.multiple_of(t * nq, nq), nq), :].astype(BF16)
        return carry

    lax.fori_loop(0, T, per_token, 0, unroll=8)


def _token_tiles_to_rows(tiles_ref, stage_ref, dtype):
    T, nq, _ = tiles_ref.shape

    def per_token(t, carry):
        stage_ref[pl.ds(pl.multiple_of(t * nq, nq), nq), :] = tiles_ref[t].astype(F32)
        return carry

    lax.fori_loop(0, T, per_token, 0, unroll=8)
    return jnp.concatenate([stage_ref[pl.ds(q, T, stride=nq), :].astype(dtype) for q in range(nq)],
                           axis=1)


def _outproj_kernel(x_ref, ym_ref, yr_ref, mod_ref, wm_ref, wr_ref, wrt_ref, brt_ref,
                    x1_ref, hp_ref, lt_ref, stage_ref):
    mix = (jnp.dot(ym_ref[...], wm_ref[...], preferred_element_type=F32)
           + jnp.dot(yr_ref[...], wr_ref[...], preferred_element_type=F32))
    x1 = x_ref[...] + mod_ref[0, 2:3, :] * mix
    x1_ref[...] = x1
    ms = jnp.mean(x1 * x1, axis=-1, keepdims=True)
    hn = x1 * lax.rsqrt(ms + EPS) * (1.0 + mod_ref[0, 4:5, :]) + mod_ref[0, 3:4, :]
    lt_ref[...] = lax.dot_general(wrt_ref[...], hn.astype(BF16), _NT,
                                  preferred_element_type=F32) + brt_ref[:, 0:1]
    _rows_to_token_tiles(hn, stage_ref, hp_ref)


def _out_proj(x2, ym2, yr2, mod3, w_out_b, w_rt_t, b_rt, S, tm):
    N, D = x2.shape
    MW = ym2.shape[1]
    RW = yr2.shape[1]
    assert MW == RW, "the two head groups share one row-block size of w_out"
    NR = w_rt_t.shape[0]
    spb = S // tm
    return pl.pallas_call(
        _outproj_kernel,
        grid=(N // tm,),
        in_specs=[pl.BlockSpec((tm, D), lambda i: (i, 0)),
                  pl.BlockSpec((tm, MW), lambda i: (i, 0)),
                  pl.BlockSpec((tm, RW), lambda i: (i, 0)),
                  pl.BlockSpec((1, 6, D), lambda i: (i // spb, 0, 0)),
                  pl.BlockSpec((MW, D), lambda i: (0, 0)),
                  pl.BlockSpec((RW, D), lambda i: (1, 0)),
                  pl.BlockSpec((NR, D), lambda i: (0, 0)),
                  pl.BlockSpec((NR, LANES), lambda i: (0, 0))],
        out_specs=[pl.BlockSpec((tm, D), lambda i: (i, 0)),
                   pl.BlockSpec((tm, D // LANES, LANES), lambda i: (i, 0, 0)),
                   pl.BlockSpec((NR, tm), lambda i: (0, i))],
        out_shape=[jax.ShapeDtypeStruct((N, D), F32),
                   jax.ShapeDtypeStruct((N, D // LANES, LANES), BF16),
                   jax.ShapeDtypeStruct((NR, N), F32)],
        scratch_shapes=[pltpu.VMEM((tm * (D // LANES), LANES), F32)],
        compiler_params=_cparams(("arbitrary",)),
        name="out_proj",
    )(x2, ym2, yr2, mod3, w_out_b, w_out_b, w_rt_t, b_rt)


def _router_kernel(lt_ref, ri_ref, rw_ref, meta_ref, pcol_ref, carry_ref):
    i = pl.program_id(0)
    tm = lt_ref.shape[1]
    E8 = EXPERTS_PER_GROUP

    @pl.when(i == 0)
    def _():
        carry_ref[...] = jnp.zeros_like(carry_ref)
        meta_ref[...] = jnp.zeros_like(meta_ref)
        pcol_ref[...] = jnp.zeros_like(pcol_ref)

    sub = lax.broadcasted_iota(I32, (SUBLANES, tm), 0)
    gl = jnp.where(sub < N_GROUPS, lt_ref[N_EXPERTS:N_EXPERTS + SUBLANES, :], -jnp.inf)
    ge = jnp.exp(gl - jnp.max(gl, axis=0, keepdims=True))
    pg = ge / jnp.sum(ge, axis=0, keepdims=True)
    pg_sel = jnp.max(pg, axis=0, keepdims=True)
    g_sel = jnp.min(jnp.where(pg == pg_sel, sub, SUBLANES), axis=0, keepdims=True)

    el = lt_ref[(N_GROUPS - 1) * E8:N_GROUPS * E8, :]
    for g in range(N_GROUPS - 2, -1, -1):
        el = jnp.where(g_sel == g, lt_ref[g * E8:(g + 1) * E8, :], el)
    ee = jnp.exp(el - jnp.max(el, axis=0, keepdims=True))
    pe = ee / jnp.sum(ee, axis=0, keepdims=True)
    p0 = jnp.max(pe, axis=0, keepdims=True)
    i0 = jnp.min(jnp.where(pe == p0, sub, SUBLANES), axis=0, keepdims=True)
    pe1 = jnp.where(sub == i0, -1.0, pe)
    p1 = jnp.max(pe1, axis=0, keepdims=True)
    i1 = jnp.min(jnp.where(pe1 == p1, sub, SUBLANES), axis=0, keepdims=True)
    psum = p0 + p1
    w0 = pg_sel * p0 / psum
    w1 = pg_sel * p1 / psum
    e0 = g_sel * E8 + i0
    e1 = g_sel * E8 + i1

    eid = lax.broadcasted_iota(I32, (N_EXPERTS, tm), 0)
    oh0 = (eid == e0).astype(F32)
    oh1 = (eid == e1).astype(F32)
    oh = oh0 + oh1
    r_ = lax.broadcasted_iota(I32, (tm, tm), 0)
    c_ = lax.broadcasted_iota(I32, (tm, tm), 1)
    before = (r_ < c_).astype(BF16)
    cnt = carry_ref[:, 0:1] + jnp.dot(oh.astype(BF16), before, preferred_element_type=F32)
    pos0 = jnp.sum(oh0 * cnt, axis=0, keepdims=True)
    pos1 = jnp.sum(oh1 * cnt, axis=0, keepdims=True)
    new_carry = carry_ref[...] + jnp.sum(oh, axis=1, keepdims=True)
    carry_ref[...] = new_carry

    ri = jnp.where(sub == 0, e0, 0)
    ri = jnp.where(sub == 1, e1, ri)
    ri = jnp.where(sub == 2, pos0.astype(I32), ri)
    ri = jnp.where(sub == 3, pos1.astype(I32), ri)
    ri_ref[...] = ri
    rw_ref[...] = jnp.where(sub == 0, w0, jnp.where(sub == 1, w1, 0.0))

    @pl.when(i == pl.num_programs(0) - 1)
    def _():
        nbp = meta_ref.shape[1]
        cnt_col = new_carry[:, 0:1]
        padded_col = jnp.ceil(cnt_col / MOE_ROWS) * MOE_ROWS
        nonempty = cnt_col > 0.0
        e_sub = lax.broadcasted_iota(I32, (N_EXPERTS, nbp), 0)
        lane = lax.broadcasted_iota(I32, (N_EXPERTS, nbp), 1)
        padded_row = jnp.sum(jnp.where(e_sub == lane, padded_col, 0.0), axis=0, keepdims=True)
        pend_row = jnp.sum(jnp.where(e_sub <= lane, padded_col, 0.0), axis=0, keepdims=True)
        pend_col = jnp.sum(jnp.where(lane <= e_sub, padded_row, 0.0), axis=1, keepdims=True)
        blk_start = lane.astype(F32) * MOE_ROWS
        e_f = e_sub.astype(F32)
        be = jnp.sum(jnp.where(pend_col <= blk_start, 1.0, 0.0), axis=0, keepdims=True)
        be = jnp.minimum(be, N_EXPERTS - 1.0)
        nxt = jnp.min(jnp.where(jnp.logical_and(e_f > be, nonempty), e_f, float(N_EXPERTS)),
                      axis=0, keepdims=True)
        run = jnp.sum(jnp.where(jnp.logical_and(e_f < be, nonempty), 1.0, 0.0),
                      axis=0, keepdims=True)
        cnt_row = jnp.sum(jnp.where(e_sub == lane, cnt_col, 0.0), axis=0, keepdims=True)
        n_used = pend_row[:, N_EXPERTS - 1:N_EXPERTS] / MOE_ROWS
        sub8 = lax.broadcasted_iota(I32, (SUBLANES, nbp), 0)
        meta = jnp.where(sub8 == META_BLOCK_E, be, 0.0)
        meta = jnp.where(sub8 == META_NEXT_E, nxt, meta)
        meta = jnp.where(sub8 == META_SLOT, run - 2.0 * jnp.floor(run * 0.5), meta)
        meta = jnp.where(sub8 == META_NUSED, n_used, meta)
        meta = jnp.where(sub8 == META_PAD_LO, pend_row - padded_row + cnt_row, meta)
        meta = jnp.where(sub8 == META_PAD_N, padded_row - cnt_row, meta)
        meta_ref[...] = meta.astype(I32)
        pcol_ref[...] = jnp.broadcast_to(pend_col - padded_col, pcol_ref.shape)


def _router(logits_t, tm, n_blocks):
    NR, N = logits_t.shape
    nbp = max(-(-n_blocks // LANES), 1) * LANES
    return pl.pallas_call(
        _router_kernel,
        grid=(N // tm,),
        in_specs=[pl.BlockSpec((NR, tm), lambda i: (0, i))],
        out_specs=[pl.BlockSpec((SUBLANES, tm), lambda i: (0, i)),
                   pl.BlockSpec((SUBLANES, tm), lambda i: (0, i)),
                   pl.BlockSpec((SUBLANES, nbp), lambda i: (0, 0)),
                   pl.BlockSpec((N_EXPERTS, LANES), lambda i: (0, 0))],
        out_shape=[jax.ShapeDtypeStruct((SUBLANES, N), I32),
                   jax.ShapeDtypeStruct((SUBLANES, N), F32),
                   jax.ShapeDtypeStruct((SUBLANES, nbp), I32),
                   jax.ShapeDtypeStruct((N_EXPERTS, LANES), F32)],
        scratch_shapes=[pltpu.VMEM((N_EXPERTS, LANES), F32)],
        compiler_params=_cparams(("arbitrary",)),
        name="router",
    )(logits_t)


def _slots_kernel(ri_ref, pcol_ref, d_ref):
    tm = ri_ref.shape[1]
    eid = lax.broadcasted_iota(I32, (N_EXPERTS, tm), 0)
    pstart = pcol_ref[:, 0:1]
    sub = lax.broadcasted_iota(I32, (SUBLANES, tm), 0)
    out = jnp.zeros((SUBLANES, tm), I32)
    for kk in range(TOP_K):
        first = jnp.sum(jnp.where(eid == ri_ref[kk:kk + 1, :], pstart, 0.0), axis=0, keepdims=True)
        out = jnp.where(sub == kk, first.astype(I32) + ri_ref[TOP_K + kk:TOP_K + kk + 1, :], out)
    d_ref[...] = out


def _slots(ri, pcol, tm):
    _, N = ri.shape
    return pl.pallas_call(
        _slots_kernel,
        grid=(N // tm,),
        in_specs=[pl.BlockSpec((SUBLANES, tm), lambda i: (0, i)),
                  pl.BlockSpec((N_EXPERTS, LANES), lambda i: (0, 0))],
        out_specs=pl.BlockSpec((SUBLANES, tm), lambda i: (0, i)),
        out_shape=jax.ShapeDtypeStruct((SUBLANES, N), I32),
        compiler_params=_cparams(("arbitrary",)),
        name="slots",
    )(ri, pcol)


ROW_DMA_UNROLL = 8


def _row_copy(src, dst, sem):
    return pltpu.make_async_copy(src, dst, sem)


def _unrolled(n, fn):
    def trip(g, carry):
        for u in range(ROW_DMA_UNROLL):
            fn(g * ROW_DMA_UNROLL + u)
        return carry
    lax.fori_loop(0, n // ROW_DMA_UNROLL, trip, 0)


def _dispatch_kernel(meta_ref, dest_ref, hp_ref, xs_ref, zero_ref, sem):
    i = pl.program_id(0)
    tm = hp_ref.shape[0]
    n_blocks = xs_ref.shape[0] // MOE_ROWS

    def issue(t):
        for kk in range(TOP_K):
            _row_copy(hp_ref.at[t], xs_ref.at[dest_ref[kk, t]], sem.at[0]).start(priority=kk)

    _unrolled(tm, issue)
    _unrolled(TOP_K * tm, lambda t: _row_copy(hp_ref.at[0], xs_ref.at[0], sem.at[0]).wait())

    @pl.when(i == pl.num_programs(0) - 1)
    def _():
        zero_ref[...] = jnp.zeros_like(zero_ref)
        for e in range(N_EXPERTS):
            lo = meta_ref[META_PAD_LO, e]
            n_pad = meta_ref[META_PAD_N, e]

            def fill(r, carry):
                _row_copy(zero_ref.at[0], xs_ref.at[lo + r], sem.at[1]).start()
                return carry

            def drain(r, carry):
                _row_copy(zero_ref.at[0], xs_ref.at[0], sem.at[1]).wait()
                return carry

            lax.fori_loop(0, n_pad, fill, 0)
            lax.fori_loop(0, n_pad, drain, 0)

        def block_copy(b):
            return _row_copy(zero_ref, xs_ref.at[pl.ds(b * MOE_ROWS, MOE_ROWS)], sem.at[1])

        n_used = meta_ref[META_NUSED, 0]
        lax.fori_loop(n_used, n_blocks, lambda b, c: (block_copy(b).start(), c)[1], 0)
        lax.fori_loop(n_used, n_blocks, lambda b, c: (block_copy(0).wait(), c)[1], 0)


def _dispatch(meta, dest, hp, n_slots, tm):
    N, nq, _ = hp.shape
    grid_spec = pltpu.PrefetchScalarGridSpec(
        num_scalar_prefetch=1,
        grid=(N // tm,),
        in_specs=[pl.BlockSpec((SUBLANES, tm), lambda i, m: (0, i), memory_space=pltpu.SMEM),
                  pl.BlockSpec((tm, nq, LANES), lambda i, m: (i, 0, 0))],
        out_specs=pl.BlockSpec(memory_space=pl.ANY),
        scratch_shapes=[pltpu.VMEM((MOE_ROWS, nq, LANES), BF16), pltpu.SemaphoreType.DMA((2,))],
    )
    return pl.pallas_call(
        _dispatch_kernel,
        grid_spec=grid_spec,
        out_shape=jax.ShapeDtypeStruct((n_slots, nq, LANES), BF16),
        compiler_params=_cparams(("arbitrary",)),
        name="dispatch",
    )(meta, dest, hp)


def _expert_kernel(meta_ref, xs_ref, wg_hbm, wu_hbm, wd_hbm, ys_ref, wbuf_g, wbuf_u, wbuf_d,
                   wgu_s, wd_s, stage_ref, sem):
    i = pl.program_id(0)
    DE = wbuf_g.shape[2]
    e = meta_ref[META_BLOCK_E, i]
    e_prev = meta_ref[META_BLOCK_E, jnp.maximum(i - 1, 0)]
    used = i < meta_ref[META_NUSED, 0]
    first = jnp.logical_and(used, jnp.logical_or(i == 0, e != e_prev))
    slot = meta_ref[META_SLOT, i]
    nxt = meta_ref[META_NEXT_E, i]

    def weight_copies(ex, s):
        return (_row_copy(wg_hbm.at[ex], wbuf_g.at[s], sem.at[s]),
                _row_copy(wu_hbm.at[ex], wbuf_u.at[s], sem.at[s]),
                _row_copy(wd_hbm.at[ex], wbuf_d.at[s], sem.at[s]))

    @pl.when(jnp.logical_and(used, i == 0))
    def _():
        for cp in weight_copies(e, slot):
            cp.start()

    @pl.when(first)
    def _():
        @pl.when(nxt < N_EXPERTS)
        def _():
            for cp in weight_copies(nxt, 1 - slot):
                cp.start()

        for cp in weight_copies(e, slot):
            cp.wait()
        wgu_s[:, :DE] = wbuf_g[slot].astype(BF16)
        wgu_s[:, DE:] = wbuf_u[slot].astype(BF16)
        wd_s[...] = wbuf_d[slot].astype(BF16)

    @pl.when(used)
    def _():
        xb = _token_tiles_to_rows(xs_ref, stage_ref, BF16)
        gu = jnp.dot(xb, wgu_s[...], preferred_element_type=F32)
        hb = _silu(gu[:, :DE]) * gu[:, DE:]
        ys_ref[...] = jnp.dot(hb.astype(BF16), wd_s[...], preferred_element_type=F32)

    @pl.when(jnp.logical_not(used))
    def _():
        ys_ref[...] = jnp.zeros_like(ys_ref)


def _experts(meta, xs, w_e_gate, w_e_up, w_e_down):
    n_slots, nq, _ = xs.shape
    _, D, DE = w_e_gate.shape
    n_blocks = n_slots // MOE_ROWS
    hbm = pl.BlockSpec(memory_space=pl.ANY)
    grid_spec = pltpu.PrefetchScalarGridSpec(
        num_scalar_prefetch=1,
        grid=(n_blocks,),
        in_specs=[pl.BlockSpec((MOE_ROWS, nq, LANES), lambda i, m: (i, 0, 0)), hbm, hbm, hbm],
        out_specs=pl.BlockSpec((MOE_ROWS, D), lambda i, m: (i, 0)),
        scratch_shapes=[pltpu.VMEM((2, D, DE), F32), pltpu.VMEM((2, D, DE), F32),
                        pltpu.VMEM((2, DE, D), F32),
                        pltpu.VMEM((D, 2 * DE), BF16), pltpu.VMEM((DE, D), BF16),
                        pltpu.VMEM((MOE_ROWS * nq, LANES), F32),
                        pltpu.SemaphoreType.DMA((2,))],
    )
    return pl.pallas_call(
        _expert_kernel,
        grid_spec=grid_spec,
        out_shape=jax.ShapeDtypeStruct((n_slots, D), F32),
        compiler_params=_cparams(("arbitrary",)),
        name="experts",
    )(meta, xs, w_e_gate, w_e_up, w_e_down)


def _combine_kernel(rcur_ref, rnext_ref, x1_ref, wt_ref, mod_ref, fg_ref, ys_ref, o_ref, gbuf, sem):
    i = pl.program_id(0)
    n = pl.num_programs(0)
    tm = x1_ref.shape[0]
    slot = i % 2

    def gather(dest_ref, s):
        def issue(t):
            for kk in range(TOP_K):
                _row_copy(ys_ref.at[pl.ds(dest_ref[kk, t], 1)], gbuf.at[s, kk, pl.ds(t, 1)],
                          sem.at[s]).start(priority=kk)
        _unrolled(tm, issue)

    @pl.when(i == 0)
    def _():
        gather(rcur_ref, 0)

    @pl.when(i + 1 < n)
    def _():
        gather(rnext_ref, 1 - slot)

    _unrolled(TOP_K * tm,
              lambda t: _row_copy(ys_ref.at[pl.ds(0, 1)], gbuf.at[slot, 0, pl.ds(0, 1)],
                                  sem.at[slot]).wait())

    wt = wt_ref[...]
    y = wt[:, 0:1] * gbuf[slot, 0] + wt[:, 1:2] * gbuf[slot, 1]
    x2 = x1_ref[...] + mod_ref[0, 5:6, :] * y
    ms = jnp.mean(x2 * x2, axis=-1, keepdims=True)
    o_ref[...] = x2 * lax.rsqrt(ms + EPS) * fg_ref[...]


def _combine(dest, x1, wts, mod3, final_g, ys, S, tm):
    N, D = x1.shape
    spb = S // tm
    nsteps = N // tm
    return pl.pallas_call(
        _combine_kernel,
        grid=(nsteps,),
        in_specs=[pl.BlockSpec((SUBLANES, tm), lambda i: (0, i), memory_space=pltpu.SMEM),
                  pl.BlockSpec((SUBLANES, tm), lambda i: (0, jnp.minimum(i + 1, nsteps - 1)),
                               memory_space=pltpu.SMEM),
                  pl.BlockSpec((tm, D), lambda i: (i, 0)),
                  pl.BlockSpec((tm, SUBLANES), lambda i: (i, 0)),
                  pl.BlockSpec((1, 6, D), lambda i: (i // spb, 0, 0)),
                  pl.BlockSpec((1, D), lambda i: (0, 0)),
                  pl.BlockSpec(memory_space=pl.ANY)],
        out_specs=pl.BlockSpec((tm, D), lambda i: (i, 0)),
        out_shape=jax.ShapeDtypeStruct((N, D), F32),
        scratch_shapes=[pltpu.VMEM((2, TOP_K, tm, D), F32), pltpu.SemaphoreType.DMA((2,))],
        compiler_params=_cparams(("arbitrary",)),
        name="combine",
    )(dest, dest, x1, wts, mod3, final_g.reshape(1, D), ys)


def _tiles(S):
    def fit(t):
        return min(t, S)
    return dict(prep_tr=512, inproj_tm=fit(1024), inproj_tn=1024, chunk=fit(256), lru_t=fit(256),
                outproj_tm=fit(512), router_tm=fit(512), dispatch_tm=fit(256), combine_tm=fit(256))


def _layer(x, mod, w_in, b_gates, conv_qk, mh_norm_g, lru_conv_w, lru_conv_b, w_lru_a, b_lru_a,
           w_lru_x, b_lru_x, lru_lambda, lru_norm_g, w_out, w_group, b_group, w_router, b_router,
           w_e_gate, w_e_up, w_e_down, final_g):
    B, S, D = x.shape
    N = B * S
    MW = mh_norm_g.shape[-1]
    RW = lru_lambda.shape[-1]
    H = M_HEADS
    tl = _tiles(S)
    x2 = x.reshape(N, D)
    mod3 = mod.reshape(B, 6, D)

    n_qkvo = 4 * MW
    w_in_t = w_in.T
    w_main_t = _prep_w_in(w_in_t, n_qkvo, tl["prep_tr"])
    w_gate_t = w_in_t[n_qkvo:n_qkvo + 2 * H]
    w_ift = jnp.zeros((SUBLANES, D), F32).at[:2 * H].set(w_gate_t).astype(BF16)
    w_if_t = jnp.zeros((LANES, D), F32).at[:2 * H].set(w_gate_t).astype(BF16)
    proj, gates, gates_t = _in_proj(x2, mod3, w_main_t, w_if_t, w_ift, S, tl["inproj_tm"],
                                    tl["inproj_tn"])
    proj3 = proj.reshape(B, S, -1)

    bg_row = jnp.zeros((1, LANES), F32).at[0, :2 * H].set(b_gates)
    bg_col = jnp.zeros((SUBLANES, LANES), F32).at[:2 * H, :].set(b_gates[:, None])
    ym = _mlstm(proj3, gates.reshape(B, S, LANES), gates_t, bg_row, bg_col, conv_qk,
                mh_norm_g.reshape(1, MW), tl["chunk"])

    wax = jnp.concatenate([w_lru_a, w_lru_x], axis=-1).astype(BF16)
    yr = _rglru(proj3, lru_conv_w, lru_conv_b.reshape(1, RW), wax, b_lru_a.reshape(1, RW),
                b_lru_x.reshape(1, RW), lru_lambda.reshape(1, RW), lru_norm_g.reshape(1, RW),
                tl["lru_t"], n_qkvo // RW, n_qkvo // RW + 1)

    NR = N_EXPERTS + SUBLANES
    w_rt_t = (jnp.zeros((NR, D), F32).at[:N_EXPERTS].set(w_router.T)
              .at[N_EXPERTS:N_EXPERTS + N_GROUPS].set(w_group.T)).astype(BF16)
    b_rt = (jnp.zeros((NR, LANES), F32).at[:N_EXPERTS, :].set(b_router[:, None])
            .at[N_EXPERTS:N_EXPERTS + N_GROUPS, :].set(b_group[:, None]))
    x1, hp, logits_t = _out_proj(x2, ym.reshape(N, MW), yr.reshape(N, RW), mod3,
                                 w_out.astype(BF16), w_rt_t, b_rt, S, tl["outproj_tm"])

    n_slots = N * TOP_K + N_EXPERTS * MOE_ROWS
    ri, rw, meta, pcol = _router(logits_t, tl["router_tm"], n_slots // MOE_ROWS)
    dest = _slots(ri, pcol, tl["router_tm"])
    xs = _dispatch(meta, dest, hp, n_slots, tl["dispatch_tm"])
    ys = _experts(meta, xs, w_e_gate, w_e_up, w_e_down)
    out = _combine(dest, x1, rw.T, mod3, final_g, ys, S, tl["combine_tm"])
    return out.reshape(B, S, D)


def kernel(x, c, w_ada, b_ada, w_in, b_gates, conv_qk, mh_norm_g, lru_conv_w, lru_conv_b, w_lru_a, b_lru_a, w_lru_x, b_lru_x, lru_lambda, lru_norm_g, w_out, w_group, b_group, w_router, b_router, w_e_gate, w_e_up, w_e_down, final_g):
    depth = w_ada.shape[0]
    assert depth == 1, "single trunk layer"
    l = 0
    mod = _ada_mod(c, w_ada[l], b_ada[l])
    return _layer(x, mod, w_in[l], b_gates[l], conv_qk[l], mh_norm_g[l], lru_conv_w[l],
                  lru_conv_b[l], w_lru_a[l], b_lru_a[l], w_lru_x[l], b_lru_x[l], lru_lambda[l],
                  lru_norm_g[l], w_out[l], w_group[l], b_group[l], w_router[l], b_router[l],
                  w_e_gate[l], w_e_up[l], w_e_down[l], final_g)
```

```python
import functools

import jax
import jax.numpy as jnp
from jax import lax
from jax.experimental import pallas as pl
from jax.experimental.pallas import tpu as pltpu

F32 = jnp.float32
BF16 = jnp.bfloat16
U32 = jnp.uint32
I32 = jnp.int32

EPS = 1e-6
M_HEADS = 4
R_BLOCKS = 8
CONV_WIDTH = 4
LRU_C = 8.0
N_GROUPS = 4
EXPERTS_PER_GROUP = 8
N_EXPERTS = N_GROUPS * EXPERTS_PER_GROUP
TOP_K = 2

LANES = 128
SUBLANES = 8
VMEM_LIMIT = 56 * 1024 * 1024
PROLOGUE_ROWS = 256

MOE_ROWS = 256
CAST_ROWS = 256
META_BLOCK_E, META_NEXT_E, META_SLOT, META_NUSED, META_PAD_LO, META_PAD_N = range(6)


def _cparams(sem, vmem=VMEM_LIMIT):
    return pltpu.CompilerParams(dimension_semantics=sem, vmem_limit_bytes=vmem)


def _log_sigmoid(x):
    return jnp.minimum(x, 0.0) - jnp.log1p(jnp.exp(-jnp.abs(x)))


def _silu(x):
    return x * jax.nn.sigmoid(x)


def _split3(x):
    hi = x.astype(BF16)
    r1 = x - hi.astype(F32)
    mid = r1.astype(BF16)
    lo = (r1 - mid.astype(F32)).astype(BF16)
    return hi, mid, lo


def _causal_conv(cur, prev8, w4):
    T, C = cur.shape
    G = T // SUBLANES
    x3 = cur.reshape(G, SUBLANES, C)
    p3 = prev8.reshape(1, SUBLANES, C)
    rid = lax.broadcasted_iota(I32, x3.shape, 1)
    acc = x3 * w4[CONV_WIDTH - 1:CONV_WIDTH, :]
    for d in range(1, CONV_WIDTH):
        rot = pltpu.roll(x3, d, axis=1)
        before = jnp.concatenate([pltpu.roll(p3, d, axis=1), rot[:G - 1]], axis=0)
        acc = acc + jnp.where(rid < d, before, rot) * w4[CONV_WIDTH - 1 - d:CONV_WIDTH - d, :]
    return acc.reshape(T, C)


def _ada_kernel(c_ref, w_ref, b_ref, o_ref):
    s = _silu(c_ref[...])
    o_ref[...] = jnp.dot(s.astype(BF16), w_ref[...].astype(BF16),
                         preferred_element_type=F32) + b_ref[...]


def _ada_mod(c, w_ada, b_ada):
    B, D = c.shape
    n6 = w_ada.shape[1]
    tn = 1024
    cp = jnp.zeros((SUBLANES, D), F32).at[:B].set(c)
    out = pl.pallas_call(
        _ada_kernel,
        grid=(n6 // tn,),
        in_specs=[pl.BlockSpec((SUBLANES, D), lambda j: (0, 0)),
                  pl.BlockSpec((D, tn), lambda j: (0, j)),
                  pl.BlockSpec((1, tn), lambda j: (0, j))],
        out_specs=pl.BlockSpec((SUBLANES, tn), lambda j: (0, j)),
        out_shape=jax.ShapeDtypeStruct((SUBLANES, n6), F32),
        compiler_params=_cparams(("arbitrary",)),
        name="ada_mod",
    )(cp, w_ada, b_ada.reshape(1, n6))
    return out[:B]


_NT = (((1,), (1,)), ((), ()))


def _prep_kernel(w_ref, wm_ref):
    wm_ref[...] = w_ref[...].astype(BF16)


def _prep_w_in(w_in_t, n_qkvo, tr):
    nin, D = w_in_t.shape
    n_gate = 2 * M_HEADS
    nm = nin - n_gate
    assert n_qkvo % tr == 0 and nm % tr == 0 and n_gate % SUBLANES == 0
    return pl.pallas_call(
        _prep_kernel,
        grid=(nm // tr,),
        in_specs=[pl.BlockSpec((pl.Element(tr), pl.Element(D)),
                               lambda i: (pl.multiple_of(
                                   i * tr + jnp.where(i * tr < n_qkvo, 0, n_gate), SUBLANES), 0))],
        out_specs=pl.BlockSpec((tr, D), lambda i: (i, 0)),
        out_shape=jax.ShapeDtypeStruct((nm, D), BF16),
        compiler_params=_cparams(("arbitrary",)),
        name="prep_w_in",
    )(w_in_t)


def _inproj_kernel(x_ref, mod_ref, w_ref, wif_ref, wift_ref, o_ref, g_ref, gt_ref, hn_ref):
    j = pl.program_id(1)

    @pl.when(j == 0)
    def _():
        tm = x_ref.shape[0]
        rc = min(tm, PROLOGUE_ROWS)
        for r in range(tm // rc):
            rows = slice(r * rc, (r + 1) * rc)
            x = x_ref[rows, :]
            ms = jnp.mean(x * x, axis=-1, keepdims=True)
            hn = x * lax.rsqrt(ms + EPS) * (1.0 + mod_ref[0, 1:2, :]) + mod_ref[0, 0:1, :]
            hb = hn.astype(BF16)
            hn_ref[rows, :] = hb
            g_ref[rows, :] = lax.dot_general(hb, wif_ref[...], _NT, preferred_element_type=F32)
            gt_ref[:, rows] = lax.dot_general(wift_ref[...], hb, _NT, preferred_element_type=F32)

    o_ref[...] = lax.dot_general(hn_ref[...], w_ref[...], _NT, preferred_element_type=F32)


def _in_proj(x2, mod3, w_main_t, w_if_t, w_ift, S, tm, tn):
    N, D = x2.shape
    nw = w_main_t.shape[0]
    spb = S // tm
    return pl.pallas_call(
        _inproj_kernel,
        grid=(N // tm, nw // tn),
        in_specs=[pl.BlockSpec((tm, D), lambda i, j: (i, 0)),
                  pl.BlockSpec((1, 6, D), lambda i, j: (i // spb, 0, 0)),
                  pl.BlockSpec((tn, D), lambda i, j: (j, 0)),
                  pl.BlockSpec((LANES, D), lambda i, j: (0, 0)),
                  pl.BlockSpec((SUBLANES, D), lambda i, j: (0, 0))],
        out_specs=[pl.BlockSpec((tm, tn), lambda i, j: (i, j)),
                   pl.BlockSpec((tm, LANES), lambda i, j: (i, 0)),
                   pl.BlockSpec((SUBLANES, tm), lambda i, j: (0, i))],
        out_shape=[jax.ShapeDtypeStruct((N, nw), F32),
                   jax.ShapeDtypeStruct((N, LANES), F32),
                   jax.ShapeDtypeStruct((SUBLANES, N), F32)],
        scratch_shapes=[pltpu.VMEM((tm, D), BF16)],
        compiler_params=_cparams(("arbitrary", "arbitrary")),
        name="in_proj",
    )(x2, mod3, w_main_t, w_if_t, w_ift)


def _mlstm_kernel(qp_ref, kp_ref, q_ref, k_ref, v_ref, o_ref, g_ref, gt_ref, bgr_ref, bgc_ref,
                  cw_ref, ng_ref, y_ref, C_ref, n_ref, m_ref):
    c = pl.program_id(1)
    L = q_ref.shape[1]
    MW = q_ref.shape[2]
    dh = MW // M_HEADS
    H = M_HEADS

    @pl.when(c == 0)
    def _():
        C_ref[...] = jnp.zeros_like(C_ref)
        n_ref[...] = jnp.zeros_like(n_ref)
        m_ref[...] = jnp.zeros_like(m_ref)

    row = lax.broadcasted_iota(I32, (L, L), 0)
    col = lax.broadcasted_iota(I32, (L, L), 1)
    causal = col <= row
    tri = causal.astype(BF16)
    tri_t = (row <= col).astype(BF16)

    gb = g_ref[0] + bgr_ref[...]
    gtb = gt_ref[...] + bgc_ref[:, 0:1]
    b_col_all = sum(jnp.dot(tri, p, preferred_element_type=F32) for p in _split3(_log_sigmoid(gb)))
    b_row_all = sum(jnp.dot(p, tri_t, preferred_element_type=F32) for p in _split3(_log_sigmoid(gtb)))

    has_prev = c > 0

    for h in range(H):
        hs = slice(h * dh, (h + 1) * dh)
        ig_col = gb[:, h:h + 1]
        b_col = b_col_all[:, H + h:H + h + 1]
        ig_row = gtb[h:h + 1, :]
        b_row = b_row_all[H + h:H + h + 1, :]
        b_last = b_row[:, L - 1:L]
        m_prev = m_ref[h][0:1, 0:1]

        qprev = jnp.where(has_prev, qp_ref[0, :, hs], 0.0)
        kprev = jnp.where(has_prev, kp_ref[0, :, hs], 0.0)
        q = _silu(_causal_conv(q_ref[0, :, hs], qprev, cw_ref[:, hs]))
        k = _silu(_causal_conv(k_ref[0, :, hs], kprev,
                               cw_ref[:, MW + h * dh:MW + (h + 1) * dh])) * (dh ** -0.5)
        qb = q.astype(BF16)
        kb = k.astype(BF16)
        vb = v_ref[0, :, hs].astype(BF16)

        qk = lax.dot_general(qb, kb, (((1,), (1,)), ((), ())), preferred_element_type=F32)
        dmat = jnp.where(causal, b_col - b_row + ig_row, -jnp.inf)
        inter = b_col + m_prev
        m_t = jnp.maximum(inter, jnp.max(dmat, axis=-1, keepdims=True))
        s = qk * jnp.exp(dmat - m_t)
        e_inter = jnp.exp(inter - m_t)
        C_old = C_ref[h]
        n_old = n_ref[h]
        num = (jnp.dot(s.astype(BF16), vb, preferred_element_type=F32)
               + e_inter * jnp.dot(qb, C_old.astype(BF16), preferred_element_type=F32))
        den = (jnp.sum(s, axis=-1, keepdims=True)
               + e_inter * jnp.sum(q * n_old, axis=-1, keepdims=True))
        hval = num / jnp.maximum(jnp.abs(den), jnp.exp(-m_t))

        g_col = b_last - b_col + ig_col
        g_row = b_last - b_row + ig_row
        m_new = jnp.maximum(b_last + m_prev, jnp.max(g_row, axis=-1, keepdims=True))
        wk = jnp.exp(g_col - m_new)
        decay = jnp.exp(b_last + m_prev - m_new)
        kw = k * wk
        C_ref[h] = decay * C_old + lax.dot_general(kw.astype(BF16), vb, (((0,), (0,)), ((), ())),
                                                   preferred_element_type=F32)
        n_ref[h] = decay * n_old + jnp.sum(kw, axis=0, keepdims=True)
        m_ref[h] = jnp.broadcast_to(m_new, m_ref.shape[1:])

        hnorm = hval * lax.rsqrt(jnp.mean(hval * hval, axis=-1, keepdims=True) + EPS)
        ym = hnorm * ng_ref[:, hs] * jax.nn.sigmoid(o_ref[0, :, hs])
        y_ref[0, :, hs] = ym.astype(BF16)


def _mlstm(proj3, gates3, gates_t, bg_row, bg_col, conv_qk, mh_norm_g, L):
    B, S, _ = proj3.shape
    MW = mh_norm_g.shape[-1]
    dh = MW // M_HEADS
    nc = S // L
    l8 = L // SUBLANES

    def prev_map(colblk):
        return lambda b, c: (b, jnp.maximum(c * l8 - 1, 0), colblk)

    def cur_map(colblk):
        return lambda b, c: (b, c, colblk)

    return pl.pallas_call(
        _mlstm_kernel,
        grid=(B, nc),
        in_specs=[pl.BlockSpec((1, SUBLANES, MW), prev_map(0)),
                  pl.BlockSpec((1, SUBLANES, MW), prev_map(1)),
                  pl.BlockSpec((1, L, MW), cur_map(0)),
                  pl.BlockSpec((1, L, MW), cur_map(1)),
                  pl.BlockSpec((1, L, MW), cur_map(2)),
                  pl.BlockSpec((1, L, MW), cur_map(3)),
                  pl.BlockSpec((1, L, LANES), lambda b, c: (b, c, 0)),
                  pl.BlockSpec((SUBLANES, L), lambda b, c: (0, b * nc + c)),
                  pl.BlockSpec((1, LANES), lambda b, c: (0, 0)),
                  pl.BlockSpec((SUBLANES, LANES), lambda b, c: (0, 0)),
                  pl.BlockSpec((CONV_WIDTH, 2 * MW), lambda b, c: (0, 0)),
                  pl.BlockSpec((1, MW), lambda b, c: (0, 0))],
        out_specs=pl.BlockSpec((1, L, MW), lambda b, c: (b, c, 0)),
        out_shape=jax.ShapeDtypeStruct((B, S, MW), BF16),
        scratch_shapes=[pltpu.VMEM((M_HEADS, dh, dh), F32),
                        pltpu.VMEM((M_HEADS, 1, dh), F32),
                        pltpu.VMEM((M_HEADS, SUBLANES, LANES), F32)],
        compiler_params=_cparams(("arbitrary", "arbitrary")),
        name="mlstm",
    )(proj3, proj3, proj3, proj3, proj3, proj3, gates3, gates_t, bg_row, bg_col, conv_qk,
      mh_norm_g)


def _lru_scan(a, u, h0):
    T, C = a.shape
    G = T // SUBLANES
    a = a.reshape(G, SUBLANES, C)
    u = u.reshape(G, SUBLANES, C)
    rid = lax.broadcasted_iota(I32, a.shape, 1)
    k = 1
    while k < SUBLANES:
        keep = rid >= k
        a_sh = jnp.where(keep, pltpu.roll(a, k, axis=1), 1.0)
        u_sh = jnp.where(keep, pltpu.roll(u, k, axis=1), 0.0)
        u = a * u_sh + u
        a = a * a_sh
        k *= 2
    h = h0
    groups = []
    for g in range(G):
        blk = u[g] + a[g] * h
        groups.append(blk)
        h = blk[SUBLANES - 1:SUBLANES, :]
    return jnp.concatenate(groups, axis=0)


def _rglru_kernel(xp_ref, x_ref, gr_ref, cw_ref, cb_ref, wax_ref, ba_ref, bx_ref, lam_ref, ng_ref,
                  y_ref, h_ref):
    t = pl.program_id(1)
    T = x_ref.shape[1]
    RW = x_ref.shape[2]
    bd = RW // R_BLOCKS

    @pl.when(t == 0)
    def _():
        h_ref[...] = jnp.zeros_like(h_ref)

    prev = jnp.where(t > 0, xp_ref[0], 0.0)
    xr = _causal_conv(x_ref[0], prev, cw_ref[...]) + cb_ref[...]
    xrb = xr.astype(BF16)
    ls = _log_sigmoid(lam_ref[...])
    for n in range(R_BLOCKS):
        sl = slice(n * bd, (n + 1) * bd)
        z = jnp.dot(xrb[:, sl], wax_ref[n], preferred_element_type=F32)
        r_gate = jax.nn.sigmoid(z[:, :bd] + ba_ref[:, sl])
        i_gate = jax.nn.sigmoid(z[:, bd:] + bx_ref[:, sl])
        log_a = LRU_C * r_gate * ls[:, sl]
        a = jnp.exp(log_a)
        u = jnp.sqrt(-jnp.tanh(log_a) * (a * a + 1.0)) * (i_gate * xr[:, sl])
        hseq = _lru_scan(a, u, h_ref[:, sl])
        h_ref[:, sl] = hseq[T - 1:T, :]
        y = hseq * jax.nn.gelu(gr_ref[0, :, sl])
        y = y * lax.rsqrt(jnp.mean(y * y, axis=-1, keepdims=True) + EPS) * ng_ref[:, sl]
        y_ref[0, :, sl] = y.astype(BF16)


def _rglru(proj3, lru_conv_w, lru_conv_b, wax, b_a, b_x, lam, ng, T, xr_blk, gr_blk):
    B, S, _ = proj3.shape
    RW = lam.shape[-1]
    bd = RW // R_BLOCKS
    t8 = T // SUBLANES
    vec = pl.BlockSpec((1, RW), lambda b, t: (0, 0))
    return pl.pallas_call(
        _rglru_kernel,
        grid=(B, S // T),
        in_specs=[pl.BlockSpec((1, SUBLANES, RW), lambda b, t: (b, jnp.maximum(t * t8 - 1, 0), xr_blk)),
                  pl.BlockSpec((1, T, RW), lambda b, t: (b, t, xr_blk)),
                  pl.BlockSpec((1, T, RW), lambda b, t: (b, t, gr_blk)),
                  pl.BlockSpec((CONV_WIDTH, RW), lambda b, t: (0, 0)),
                  vec,
                  pl.BlockSpec((R_BLOCKS, bd, 2 * bd), lambda b, t: (0, 0, 0)),
                  vec, vec, vec, vec],
        out_specs=pl.BlockSpec((1, T, RW), lambda b, t: (b, t, 0)),
        out_shape=jax.ShapeDtypeStruct((B, S, RW), BF16),
        scratch_shapes=[pltpu.VMEM((1, RW), F32)],
        compiler_params=_cparams(("arbitrary", "arbitrary")),
        name="rglru",
    )(proj3, proj3, proj3, lru_conv_w, lru_conv_b, wax, b_a, b_x, lam, ng)


def _rows_to_token_tiles(x, stage_ref, tiles_ref):
    T, D = x.shape
    nq = D // LANES
    for q in range(nq):
        stage_ref[pl.ds(q, T, stride=nq), :] = x[:, q * LANES:(q + 1) * LANES]

    def per_token(t, carry):
        tiles_ref[t] = stage_ref[pl.ds(pl.multiple_of(t * nq, nq), nq), :].astype(BF16)
        return carry

    lax.fori_loop(0, T, per_token, 0, unroll=8)


def _token_tiles_to_rows(tiles_ref, stage_ref, dtype):
    T, nq, _ = tiles_ref.shape

    def per_token(t, carry):
        stage_ref[pl.ds(pl.multiple_of(t * nq, nq), nq), :] = tiles_ref[t].astype(F32)
        return carry

    lax.fori_loop(0, T, per_token, 0, unroll=8)
    return jnp.concatenate([stage_ref[pl.ds(q, T, stride=nq), :].astype(dtype) for q in range(nq)],
                           axis=1)


def _outproj_kernel(x_ref, ym_ref, yr_ref, mod_ref, wm_ref, wr_ref, wrt_ref, brt_ref,
                    x1_ref, hp_ref, lt_ref, stage_ref):
    mix = (jnp.dot(ym_ref[...], wm_ref[...], preferred_element_type=F32)
           + jnp.dot(yr_ref[...], wr_ref[...], preferred_element_type=F32))
    x1 = x_ref[...] + mod_ref[0, 2:3, :] * mix
    x1_ref[...] = x1
    ms = jnp.mean(x1 * x1, axis=-1, keepdims=True)
    hn = x1 * lax.rsqrt(ms + EPS) * (1.0 + mod_ref[0, 4:5, :]) + mod_ref[0, 3:4, :]
    lt_ref[...] = lax.dot_general(wrt_ref[...], hn.astype(BF16), _NT,
                                  preferred_element_type=F32) + brt_ref[:, 0:1]
    _rows_to_token_tiles(hn, stage_ref, hp_ref)


def _out_proj(x2, ym2, yr2, mod3, w_out_b, w_rt_t, b_rt, S, tm):
    N, D = x2.shape
    MW = ym2.shape[1]
    RW = yr2.shape[1]
    assert MW == RW, "the two head groups share one row-block size of w_out"
    NR = w_rt_t.shape[0]
    spb = S // tm
    return pl.pallas_call(
        _outproj_kernel,
        grid=(N // tm,),
        in_specs=[pl.BlockSpec((tm, D), lambda i: (i, 0)),
                  pl.BlockSpec((tm, MW), lambda i: (i, 0)),
                  pl.BlockSpec((tm, RW), lambda i: (i, 0)),
                  pl.BlockSpec((1, 6, D), lambda i: (i // spb, 0, 0)),
                  pl.BlockSpec((MW, D), lambda i: (0, 0)),
                  pl.BlockSpec((RW, D), lambda i: (1, 0)),
                  pl.BlockSpec((NR, D), lambda i: (0, 0)),
                  pl.BlockSpec((NR, LANES), lambda i: (0, 0))],
        out_specs=[pl.BlockSpec((tm, D), lambda i: (i, 0)),
                   pl.BlockSpec((tm, D // LANES, LANES), lambda i: (i, 0, 0)),
                   pl.BlockSpec((NR, tm), lambda i: (0, i))],
        out_shape=[jax.ShapeDtypeStruct((N, D), F32),
                   jax.ShapeDtypeStruct((N, D // LANES, LANES), BF16),
                   jax.ShapeDtypeStruct((NR, N), F32)],
        scratch_shapes=[pltpu.VMEM((tm * (D // LANES), LANES), F32)],
        compiler_params=_cparams(("arbitrary",)),
        name="out_proj",
    )(x2, ym2, yr2, mod3, w_out_b, w_out_b, w_rt_t, b_rt)


def _router_kernel(lt_ref, rw_ref, meta_ref, d_ref, ri_scr, pcol_scr, carry_ref):
    phase = pl.program_id(0)
    i = pl.program_id(1)

    @pl.when(phase == 0)
    def _():
        _route_tile(i, lt_ref, ri_scr.at[i], rw_ref, meta_ref, pcol_scr, carry_ref)

    @pl.when(phase == 1)
    def _():
        _slots_tile(ri_scr.at[i], pcol_scr, d_ref)


def _route_tile(i, lt_ref, ri_ref, rw_ref, meta_ref, pcol_ref, carry_ref):
    tm = lt_ref.shape[1]
    E8 = EXPERTS_PER_GROUP

    @pl.when(i == 0)
    def _():
        carry_ref[...] = jnp.zeros_like(carry_ref)
        meta_ref[...] = jnp.zeros_like(meta_ref)
        pcol_ref[...] = jnp.zeros_like(pcol_ref)

    sub = lax.broadcasted_iota(I32, (SUBLANES, tm), 0)
    gl = jnp.where(sub < N_GROUPS, lt_ref[N_EXPERTS:N_EXPERTS + SUBLANES, :], -jnp.inf)
    ge = jnp.exp(gl - jnp.max(gl, axis=0, keepdims=True))
    pg = ge / jnp.sum(ge, axis=0, keepdims=True)
    pg_sel = jnp.max(pg, axis=0, keepdims=True)
    g_sel = jnp.min(jnp.where(pg == pg_sel, sub, SUBLANES), axis=0, keepdims=True)

    el = lt_ref[(N_GROUPS - 1) * E8:N_GROUPS * E8, :]
    for g in range(N_GROUPS - 2, -1, -1):
        el = jnp.where(g_sel == g, lt_ref[g * E8:(g + 1) * E8, :], el)
    ee = jnp.exp(el - jnp.max(el, axis=0, keepdims=True))
    pe = ee / jnp.sum(ee, axis=0, keepdims=True)
    p0 = jnp.max(pe, axis=0, keepdims=True)
    i0 = jnp.min(jnp.where(pe == p0, sub, SUBLANES), axis=0, keepdims=True)
    pe1 = jnp.where(sub == i0, -1.0, pe)
    p1 = jnp.max(pe1, axis=0, keepdims=True)
    i1 = jnp.min(jnp.where(pe1 == p1, sub, SUBLANES), axis=0, keepdims=True)
    psum = p0 + p1
    w0 = pg_sel * p0 / psum
    w1 = pg_sel * p1 / psum
    e0 = g_sel * E8 + i0
    e1 = g_sel * E8 + i1

    eid = lax.broadcasted_iota(I32, (N_EXPERTS, tm), 0)
    oh0 = (eid == e0).astype(F32)
    oh1 = (eid == e1).astype(F32)
    oh = oh0 + oh1
    r_ = lax.broadcasted_iota(I32, (tm, tm), 0)
    c_ = lax.broadcasted_iota(I32, (tm, tm), 1)
    before = (r_ < c_).astype(BF16)
    cnt = carry_ref[:, 0:1] + jnp.dot(oh.astype(BF16), before, preferred_element_type=F32)
    pos0 = jnp.sum(oh0 * cnt, axis=0, keepdims=True)
    pos1 = jnp.sum(oh1 * cnt, axis=0, keepdims=True)
    new_carry = carry_ref[...] + jnp.sum(oh, axis=1, keepdims=True)
    carry_ref[...] = new_carry

    ri = jnp.where(sub == 0, e0, 0)
    ri = jnp.where(sub == 1, e1, ri)
    ri = jnp.where(sub == 2, pos0.astype(I32), ri)
    ri = jnp.where(sub == 3, pos1.astype(I32), ri)
    ri_ref[...] = ri
    rw_ref[...] = jnp.where(sub == 0, w0, jnp.where(sub == 1, w1, 0.0))

    @pl.when(i == pl.num_programs(1) - 1)
    def _():
        nbp = meta_ref.shape[1]
        cnt_col = new_carry[:, 0:1]
        padded_col = jnp.ceil(cnt_col / MOE_ROWS) * MOE_ROWS
        nonempty = cnt_col > 0.0
        e_sub = lax.broadcasted_iota(I32, (N_EXPERTS, nbp), 0)
        lane = lax.broadcasted_iota(I32, (N_EXPERTS, nbp), 1)
        padded_row = jnp.sum(jnp.where(e_sub == lane, padded_col, 0.0), axis=0, keepdims=True)
        pend_row = jnp.sum(jnp.where(e_sub <= lane, padded_col, 0.0), axis=0, keepdims=True)
        pend_col = jnp.sum(jnp.where(lane <= e_sub, padded_row, 0.0), axis=1, keepdims=True)
        blk_start = lane.astype(F32) * MOE_ROWS
        e_f = e_sub.astype(F32)
        be = jnp.sum(jnp.where(pend_col <= blk_start, 1.0, 0.0), axis=0, keepdims=True)
        be = jnp.minimum(be, N_EXPERTS - 1.0)
        nxt = jnp.min(jnp.where(jnp.logical_and(e_f > be, nonempty), e_f, float(N_EXPERTS)),
                      axis=0, keepdims=True)
        run = jnp.sum(jnp.where(jnp.logical_and(e_f < be, nonempty), 1.0, 0.0),
                      axis=0, keepdims=True)
        cnt_row = jnp.sum(jnp.where(e_sub == lane, cnt_col, 0.0), axis=0, keepdims=True)
        n_used = pend_row[:, N_EXPERTS - 1:N_EXPERTS] / MOE_ROWS
        sub8 = lax.broadcasted_iota(I32, (SUBLANES, nbp), 0)
        meta = jnp.where(sub8 == META_BLOCK_E, be, 0.0)
        meta = jnp.where(sub8 == META_NEXT_E, nxt, meta)
        meta = jnp.where(sub8 == META_SLOT, run - 2.0 * jnp.floor(run * 0.5), meta)
        meta = jnp.where(sub8 == META_NUSED, n_used, meta)
        meta = jnp.where(sub8 == META_PAD_LO, pend_row - padded_row + cnt_row, meta)
        meta = jnp.where(sub8 == META_PAD_N, padded_row - cnt_row, meta)
        meta_ref[...] = meta.astype(I32)
        pcol_ref[...] = jnp.broadcast_to(pend_col - padded_col, pcol_ref.shape)


def _router(logits_t, tm, n_blocks):
    NR, N = logits_t.shape
    nt = N // tm
    nbp = max(-(-n_blocks // LANES), 1) * LANES
    routed = lambda p, i: (0, jnp.where(p == 0, i, nt - 1))
    return pl.pallas_call(
        _router_kernel,
        grid=(2, nt),
        in_specs=[pl.BlockSpec((NR, tm), routed)],
        out_specs=[pl.BlockSpec((SUBLANES, tm), routed),
                   pl.BlockSpec((SUBLANES, nbp), lambda p, i: (0, 0)),
                   pl.BlockSpec((SUBLANES, tm), lambda p, i: (0, jnp.where(p == 0, 0, i)))],
        out_shape=[jax.ShapeDtypeStruct((SUBLANES, N), F32),
                   jax.ShapeDtypeStruct((SUBLANES, nbp), I32),
                   jax.ShapeDtypeStruct((SUBLANES, N), I32)],
        scratch_shapes=[pltpu.VMEM((nt, SUBLANES, tm), I32),
                        pltpu.VMEM((N_EXPERTS, LANES), F32),
                        pltpu.VMEM((N_EXPERTS, LANES), F32)],
        compiler_params=_cparams(("arbitrary", "arbitrary")),
        name="router",
    )(logits_t)


def _slots_tile(ri_ref, pcol_ref, d_ref):
    tm = ri_ref.shape[1]
    eid = lax.broadcasted_iota(I32, (N_EXPERTS, tm), 0)
    pstart = pcol_ref[:, 0:1]
    sub = lax.broadcasted_iota(I32, (SUBLANES, tm), 0)
    out = jnp.zeros((SUBLANES, tm), I32)
    for kk in range(TOP_K):
        first = jnp.sum(jnp.where(eid == ri_ref[kk:kk + 1, :], pstart, 0.0), axis=0, keepdims=True)
        out = jnp.where(sub == kk, first.astype(I32) + ri_ref[TOP_K + kk:TOP_K + kk + 1, :], out)
    d_ref[...] = out


ROW_DMA_UNROLL = 8


def _row_copy(src, dst, sem):
    return pltpu.make_async_copy(src, dst, sem)


def _unrolled(n, fn):
    def trip(g, carry):
        for u in range(ROW_DMA_UNROLL):
            fn(g * ROW_DMA_UNROLL + u)
        return carry
    lax.fori_loop(0, n // ROW_DMA_UNROLL, trip, 0)


def _dispatch_kernel(meta_ref, dest_ref, hp_ref, xs_ref, zero_ref, sem):
    i = pl.program_id(0)
    tm = hp_ref.shape[0]
    n_blocks = xs_ref.shape[0] // MOE_ROWS

    def issue(t):
        for kk in range(TOP_K):
            _row_copy(hp_ref.at[t], xs_ref.at[dest_ref[kk, t]], sem.at[0]).start(priority=kk)

    _unrolled(tm, issue)
    _unrolled(TOP_K * tm, lambda t: _row_copy(hp_ref.at[0], xs_ref.at[0], sem.at[0]).wait())

    @pl.when(i == pl.num_programs(0) - 1)
    def _():
        zero_ref[...] = jnp.zeros_like(zero_ref)
        for e in range(N_EXPERTS):
            lo = meta_ref[META_PAD_LO, e]
            n_pad = meta_ref[META_PAD_N, e]

            def fill(r, carry):
                _row_copy(zero_ref.at[0], xs_ref.at[lo + r], sem.at[1]).start()
                return carry

            def drain(r, carry):
                _row_copy(zero_ref.at[0], xs_ref.at[0], sem.at[1]).wait()
                return carry

            lax.fori_loop(0, n_pad, fill, 0)
            lax.fori_loop(0, n_pad, drain, 0)

        def block_copy(b):
            return _row_copy(zero_ref, xs_ref.at[pl.ds(b * MOE_ROWS, MOE_ROWS)], sem.at[1])

        n_used = meta_ref[META_NUSED, 0]
        lax.fori_loop(n_used, n_blocks, lambda b, c: (block_copy(b).start(), c)[1], 0)
        lax.fori_loop(n_used, n_blocks, lambda b, c: (block_copy(0).wait(), c)[1], 0)


def _dispatch(meta, dest, hp, n_slots, tm):
    N, nq, _ = hp.shape
    grid_spec = pltpu.PrefetchScalarGridSpec(
        num_scalar_prefetch=1,
        grid=(N // tm,),
        in_specs=[pl.BlockSpec((SUBLANES, tm), lambda i, m: (0, i), memory_space=pltpu.SMEM),
                  pl.BlockSpec((tm, nq, LANES), lambda i, m: (i, 0, 0))],
        out_specs=pl.BlockSpec(memory_space=pl.ANY),
        scratch_shapes=[pltpu.VMEM((MOE_ROWS, nq, LANES), BF16), pltpu.SemaphoreType.DMA((2,))],
    )
    return pl.pallas_call(
        _dispatch_kernel,
        grid_spec=grid_spec,
        out_shape=jax.ShapeDtypeStruct((n_slots, nq, LANES), BF16),
        compiler_params=_cparams(("arbitrary",)),
        name="dispatch",
    )(meta, dest, hp)


def _expert_kernel(meta_ref, xs_ref, wg_hbm, wu_hbm, wd_hbm, ys_ref, wbuf_g, wbuf_u, wbuf_d,
                   wgu_s, wd_s, stage_ref, sem):
    i = pl.program_id(0)
    DE = wbuf_g.shape[2]
    e = meta_ref[META_BLOCK_E, i]
    e_prev = meta_ref[META_BLOCK_E, jnp.maximum(i - 1, 0)]
    used = i < meta_ref[META_NUSED, 0]
    first = jnp.logical_and(used, jnp.logical_or(i == 0, e != e_prev))
    slot = meta_ref[META_SLOT, i]
    nxt = meta_ref[META_NEXT_E, i]

    def weight_copies(ex, s):
        return (_row_copy(wg_hbm.at[ex], wbuf_g.at[s], sem.at[s]),
                _row_copy(wu_hbm.at[ex], wbuf_u.at[s], sem.at[s]),
                _row_copy(wd_hbm.at[ex], wbuf_d.at[s], sem.at[s]))

    @pl.when(jnp.logical_and(used, i == 0))
    def _():
        for cp in weight_copies(e, slot):
            cp.start()

    @pl.when(first)
    def _():
        @pl.when(nxt < N_EXPERTS)
        def _():
            for cp in weight_copies(nxt, 1 - slot):
                cp.start()

        for cp in weight_copies(e, slot):
            cp.wait()
        xb = _token_tiles_to_rows(xs_ref, stage_ref, BF16)
        D = wgu_s.shape[0]
        gu = jnp.zeros((xb.shape[0], 2 * DE), F32)
        for kc in range(D // CAST_ROWS):
            rows = slice(kc * CAST_ROWS, (kc + 1) * CAST_ROWS)
            w_c = jnp.concatenate([wbuf_g[slot, rows, :], wbuf_u[slot, rows, :]],
                                  axis=1).astype(BF16)
            wgu_s[rows, :] = w_c
            gu = gu + jnp.dot(xb[:, rows], w_c, preferred_element_type=F32)
        hb = (_silu(gu[:, :DE]) * gu[:, DE:]).astype(BF16)
        y = jnp.zeros(ys_ref.shape, F32)
        for kc in range(DE // CAST_ROWS):
            rows = slice(kc * CAST_ROWS, (kc + 1) * CAST_ROWS)
            w_c = wbuf_d[slot, rows, :].astype(BF16)
            wd_s[rows, :] = w_c
            y = y + jnp.dot(hb[:, rows], w_c, preferred_element_type=F32)
        ys_ref[...] = y

    @pl.when(jnp.logical_and(used, jnp.logical_not(first)))
    def _():
        xb = _token_tiles_to_rows(xs_ref, stage_ref, BF16)
        gu = jnp.dot(xb, wgu_s[...], preferred_element_type=F32)
        hb = _silu(gu[:, :DE]) * gu[:, DE:]
        ys_ref[...] = jnp.dot(hb.astype(BF16), wd_s[...], preferred_element_type=F32)

    @pl.when(jnp.logical_not(used))
    def _():
        ys_ref[...] = jnp.zeros_like(ys_ref)


def _experts(meta, xs, w_e_gate, w_e_up, w_e_down):
    n_slots, nq, _ = xs.shape
    _, D, DE = w_e_gate.shape
    n_blocks = n_slots // MOE_ROWS
    hbm = pl.BlockSpec(memory_space=pl.ANY)
    grid_spec = pltpu.PrefetchScalarGridSpec(
        num_scalar_prefetch=1,
        grid=(n_blocks,),
        in_specs=[pl.BlockSpec((MOE_ROWS, nq, LANES), lambda i, m: (i, 0, 0)), hbm, hbm, hbm],
        out_specs=pl.BlockSpec((MOE_ROWS, D), lambda i, m: (i, 0)),
        scratch_shapes=[pltpu.VMEM((2, D, DE), F32), pltpu.VMEM((2, D, DE), F32),
                        pltpu.VMEM((2, DE, D), F32),
                        pltpu.VMEM((D, 2 * DE), BF16), pltpu.VMEM((DE, D), BF16),
                        pltpu.VMEM((MOE_ROWS * nq, LANES), F32),
                        pltpu.SemaphoreType.DMA((2,))],
    )
    return pl.pallas_call(
        _expert_kernel,
        grid_spec=grid_spec,
        out_shape=jax.ShapeDtypeStruct((n_slots, D), F32),
        compiler_params=_cparams(("arbitrary",)),
        name="experts",
    )(meta, xs, w_e_gate, w_e_up, w_e_down)


def _combine_kernel(rcur_ref, rnext_ref, x1_ref, wt_ref, mod_ref, fg_ref, ys_ref, o_ref, gbuf, sem):
    i = pl.program_id(0)
    n = pl.num_programs(0)
    tm = x1_ref.shape[0]
    slot = i % 2

    def gather(dest_ref, s):
        def issue(t):
            for kk in range(TOP_K):
                _row_copy(ys_ref.at[pl.ds(dest_ref[kk, t], 1)], gbuf.at[s, kk, pl.ds(t, 1)],
                          sem.at[s]).start(priority=kk)
        _unrolled(tm, issue)

    @pl.when(i == 0)
    def _():
        gather(rcur_ref, 0)

    @pl.when(i + 1 < n)
    def _():
        gather(rnext_ref, 1 - slot)

    _unrolled(TOP_K * tm,
              lambda t: _row_copy(ys_ref.at[pl.ds(0, 1)], gbuf.at[slot, 0, pl.ds(0, 1)],
                                  sem.at[slot]).wait())

    wt = wt_ref[...]
    y = wt[:, 0:1] * gbuf[slot, 0] + wt[:, 1:2] * gbuf[slot, 1]
    x2 = x1_ref[...] + mod_ref[0, 5:6, :] * y
    ms = jnp.mean(x2 * x2, axis=-1, keepdims=True)
    o_ref[...] = x2 * lax.rsqrt(ms + EPS) * fg_ref[...]


def _combine(dest, x1, wts, mod3, final_g, ys, S, tm):
    N, D = x1.shape
    spb = S // tm
    nsteps = N // tm
    return pl.pallas_call(
        _combine_kernel,
        grid=(nsteps,),
        in_specs=[pl.BlockSpec((SUBLANES, tm), lambda i: (0, i), memory_space=pltpu.SMEM),
                  pl.BlockSpec((SUBLANES, tm), lambda i: (0, jnp.minimum(i + 1, nsteps - 1)),
                               memory_space=pltpu.SMEM),
                  pl.BlockSpec((tm, D), lambda i: (i, 0)),
                  pl.BlockSpec((tm, SUBLANES), lambda i: (i, 0)),
                  pl.BlockSpec((1, 6, D), lambda i: (i // spb, 0, 0)),
                  pl.BlockSpec((1, D), lambda i: (0, 0)),
                  pl.BlockSpec(memory_space=pl.ANY)],
        out_specs=pl.BlockSpec((tm, D), lambda i: (i, 0)),
        out_shape=jax.ShapeDtypeStruct((N, D), F32),
        scratch_shapes=[pltpu.VMEM((2, TOP_K, tm, D), F32), pltpu.SemaphoreType.DMA((2,))],
        compiler_params=_cparams(("arbitrary",)),
        name="combine",
    )(dest, dest, x1, wts, mod3, final_g.reshape(1, D), ys)


def _tiles(S):
    def fit(t):
        return min(t, S)
    return dict(prep_tr=512, inproj_tm=fit(1024), inproj_tn=1024, chunk=fit(256), lru_t=fit(256),
                outproj_tm=fit(512), router_tm=fit(512), dispatch_tm=fit(256), combine_tm=fit(256))


def _layer(x, mod, w_in, b_gates, conv_qk, mh_norm_g, lru_conv_w, lru_conv_b, w_lru_a, b_lru_a,
           w_lru_x, b_lru_x, lru_lambda, lru_norm_g, w_out, w_group, b_group, w_router, b_router,
           w_e_gate, w_e_up, w_e_down, final_g):
    B, S, D = x.shape
    N = B * S
    MW = mh_norm_g.shape[-1]
    RW = lru_lambda.shape[-1]
    H = M_HEADS
    tl = _tiles(S)
    x2 = x.reshape(N, D)
    mod3 = mod.reshape(B, 6, D)

    n_qkvo = 4 * MW
    w_in_t = w_in.T
    w_main_t = _prep_w_in(w_in_t, n_qkvo, tl["prep_tr"])
    w_gate_t = w_in_t[n_qkvo:n_qkvo + 2 * H]
    w_ift = jnp.zeros((SUBLANES, D), F32).at[:2 * H].set(w_gate_t).astype(BF16)
    w_if_t = jnp.zeros((LANES, D), F32).at[:2 * H].set(w_gate_t).astype(BF16)
    proj, gates, gates_t = _in_proj(x2, mod3, w_main_t, w_if_t, w_ift, S, tl["inproj_tm"],
                                    tl["inproj_tn"])
    proj3 = proj.reshape(B, S, -1)

    bg_row = jnp.zeros((1, LANES), F32).at[0, :2 * H].set(b_gates)
    bg_col = jnp.zeros((SUBLANES, LANES), F32).at[:2 * H, :].set(b_gates[:, None])
    ym = _mlstm(proj3, gates.reshape(B, S, LANES), gates_t, bg_row, bg_col, conv_qk,
                mh_norm_g.reshape(1, MW), tl["chunk"])

    wax = jnp.concatenate([w_lru_a, w_lru_x], axis=-1).astype(BF16)
    yr = _rglru(proj3, lru_conv_w, lru_conv_b.reshape(1, RW), wax, b_lru_a.reshape(1, RW),
                b_lru_x.reshape(1, RW), lru_lambda.reshape(1, RW), lru_norm_g.reshape(1, RW),
                tl["lru_t"], n_qkvo // RW, n_qkvo // RW + 1)

    NR = N_EXPERTS + SUBLANES
    w_rt_t = (jnp.zeros((NR, D), F32).at[:N_EXPERTS].set(w_router.T)
              .at[N_EXPERTS:N_EXPERTS + N_GROUPS].set(w_group.T)).astype(BF16)
    b_rt = (jnp.zeros((NR, LANES), F32).at[:N_EXPERTS, :].set(b_router[:, None])
            .at[N_EXPERTS:N_EXPERTS + N_GROUPS, :].set(b_group[:, None]))
    x1, hp, logits_t = _out_proj(x2, ym.reshape(N, MW), yr.reshape(N, RW), mod3,
                                 w_out.astype(BF16), w_rt_t, b_rt, S, tl["outproj_tm"])

    n_slots = N * TOP_K + N_EXPERTS * MOE_ROWS
    rw, meta, dest = _router(logits_t, tl["router_tm"], n_slots // MOE_ROWS)
    xs = _dispatch(meta, dest, hp, n_slots, tl["dispatch_tm"])
    ys = _experts(meta, xs, w_e_gate, w_e_up, w_e_down)
    out = _combine(dest, x1, rw.T, mod3, final_g, ys, S, tl["combine_tm"])
    return out.reshape(B, S, D)


def kernel(x, c, w_ada, b_ada, w_in, b_gates, conv_qk, mh_norm_g, lru_conv_w, lru_conv_b, w_lru_a, b_lru_a, w_lru_x, b_lru_x, lru_lambda, lru_norm_g, w_out, w_group, b_group, w_router, b_router, w_e_gate, w_e_up, w_e_down, final_g):
    depth = w_ada.shape[0]
    assert depth == 1, "single trunk layer"
    l = 0
    mod = _ada_mod(c, w_ada[l], b_ada[l])
    return _layer(x, mod, w_in[l], b_gates[l], conv_qk[l], mh_norm_g[l], lru_conv_w[l],
                  lru_conv_b[l], w_lru_a[l], b_lru_a[l], w_lru_x[l], b_lru_x[l], lru_lambda[l],
                  lru_norm_g[l], w_out[l], w_group[l], b_group[l], w_router[l], b_router[l],
                  w_e_gate[l], w_e_up[l], w_e_down[l], final_g)
```

```python
import functools

import jax
import jax.numpy as jnp
from jax import lax
from jax.experimental import pallas as pl
from jax.experimental.pallas import tpu as pltpu

F32 = jnp.float32
BF16 = jnp.bfloat16
U32 = jnp.uint32
I32 = jnp.int32

EPS = 1e-6
M_HEADS = 4
R_BLOCKS = 8
CONV_WIDTH = 4
LRU_C = 8.0
N_GROUPS = 4
EXPERTS_PER_GROUP = 8
N_EXPERTS = N_GROUPS * EXPERTS_PER_GROUP
TOP_K = 2

LANES = 128
SUBLANES = 8
VMEM_LIMIT = 56 * 1024 * 1024
PROLOGUE_ROWS = 256

MOE_ROWS = 256
META_BLOCK_E, META_NEXT_E, META_SLOT, META_NUSED, META_PAD_LO, META_PAD_N = range(6)


def _cparams(sem, vmem=VMEM_LIMIT):
    return pltpu.CompilerParams(dimension_semantics=sem, vmem_limit_bytes=vmem)


def _log_sigmoid(x):
    return jnp.minimum(x, 0.0) - jnp.log1p(jnp.exp(-jnp.abs(x)))


def _silu(x):
    return x * jax.nn.sigmoid(x)


def _split3(x):
    hi = x.astype(BF16)
    r1 = x - hi.astype(F32)
    mid = r1.astype(BF16)
    lo = (r1 - mid.astype(F32)).astype(BF16)
    return hi, mid, lo


def _causal_conv(cur, prev8, w4):
    T, C = cur.shape
    G = T // SUBLANES
    x3 = cur.reshape(G, SUBLANES, C)
    p3 = prev8.reshape(1, SUBLANES, C)
    rid = lax.broadcasted_iota(I32, x3.shape, 1)
    acc = x3 * w4[CONV_WIDTH - 1:CONV_WIDTH, :]
    for d in range(1, CONV_WIDTH):
        rot = pltpu.roll(x3, d, axis=1)
        before = jnp.concatenate([pltpu.roll(p3, d, axis=1), rot[:G - 1]], axis=0)
        acc = acc + jnp.where(rid < d, before, rot) * w4[CONV_WIDTH - 1 - d:CONV_WIDTH - d, :]
    return acc.reshape(T, C)


def _ada_kernel(c_ref, w_ref, b_ref, o_ref):
    s = _silu(c_ref[...])
    o_ref[...] = jnp.dot(s.astype(BF16), w_ref[...].astype(BF16),
                         preferred_element_type=F32) + b_ref[...]


def _ada_mod(c, w_ada, b_ada):
    B, D = c.shape
    n6 = w_ada.shape[1]
    tn = 1024
    cp = jnp.zeros((SUBLANES, D), F32).at[:B].set(c)
    out = pl.pallas_call(
        _ada_kernel,
        grid=(n6 // tn,),
        in_specs=[pl.BlockSpec((SUBLANES, D), lambda j: (0, 0)),
                  pl.BlockSpec((D, tn), lambda j: (0, j)),
                  pl.BlockSpec((1, tn), lambda j: (0, j))],
        out_specs=pl.BlockSpec((SUBLANES, tn), lambda j: (0, j)),
        out_shape=jax.ShapeDtypeStruct((SUBLANES, n6), F32),
        compiler_params=_cparams(("arbitrary",)),
        name="ada_mod",
    )(cp, w_ada, b_ada.reshape(1, n6))
    return out[:B]


_NT = (((1,), (1,)), ((), ()))


def _prep_kernel(w_ref, wm_ref):
    wm_ref[...] = w_ref[...].astype(BF16)


def _prep_w_in(w_in_t, n_qkvo, tr):
    nin, D = w_in_t.shape
    n_gate = 2 * M_HEADS
    nm = nin - n_gate
    assert n_qkvo % tr == 0 and nm % tr == 0 and n_gate % SUBLANES == 0
    return pl.pallas_call(
        _prep_kernel,
        grid=(nm // tr,),
        in_specs=[pl.BlockSpec((pl.Element(tr), pl.Element(D)),
                               lambda i: (pl.multiple_of(
                                   i * tr + jnp.where(i * tr < n_qkvo, 0, n_gate), SUBLANES), 0))],
        out_specs=pl.BlockSpec((tr, D), lambda i: (i, 0)),
        out_shape=jax.ShapeDtypeStruct((nm, D), BF16),
        compiler_params=_cparams(("arbitrary",)),
        name="prep_w_in",
    )(w_in_t)


def _inproj_kernel(x_ref, mod_ref, w_ref, wif_ref, wift_ref, o_ref, g_ref, gt_ref, hn_ref):
    j = pl.program_id(1)

    @pl.when(j == 0)
    def _():
        tm = x_ref.shape[0]
        rc = min(tm, PROLOGUE_ROWS)
        for r in range(tm // rc):
            rows = slice(r * rc, (r + 1) * rc)
            x = x_ref[rows, :]
            ms = jnp.mean(x * x, axis=-1, keepdims=True)
            hn = x * lax.rsqrt(ms + EPS) * (1.0 + mod_ref[0, 1:2, :]) + mod_ref[0, 0:1, :]
            hb = hn.astype(BF16)
            hn_ref[rows, :] = hb
            g_ref[rows, :] = lax.dot_general(hb, wif_ref[...], _NT, preferred_element_type=F32)
            gt_ref[:, rows] = lax.dot_general(wift_ref[...], hb, _NT, preferred_element_type=F32)

    o_ref[...] = lax.dot_general(hn_ref[...], w_ref[...], _NT, preferred_element_type=F32)


def _in_proj(x2, mod3, w_main_t, w_if_t, w_ift, S, tm, tn):
    N, D = x2.shape
    nw = w_main_t.shape[0]
    spb = S // tm
    return pl.pallas_call(
        _inproj_kernel,
        grid=(N // tm, nw // tn),
        in_specs=[pl.BlockSpec((tm, D), lambda i, j: (i, 0)),
                  pl.BlockSpec((1, 6, D), lambda i, j: (i // spb, 0, 0)),
                  pl.BlockSpec((tn, D), lambda i, j: (j, 0)),
                  pl.BlockSpec((LANES, D), lambda i, j: (0, 0)),
                  pl.BlockSpec((SUBLANES, D), lambda i, j: (0, 0))],
        out_specs=[pl.BlockSpec((tm, tn), lambda i, j: (i, j)),
                   pl.BlockSpec((tm, LANES), lambda i, j: (i, 0)),
                   pl.BlockSpec((SUBLANES, tm), lambda i, j: (0, i))],
        out_shape=[jax.ShapeDtypeStruct((N, nw), F32),
                   jax.ShapeDtypeStruct((N, LANES), F32),
                   jax.ShapeDtypeStruct((SUBLANES, N), F32)],
        scratch_shapes=[pltpu.VMEM((tm, D), BF16)],
        compiler_params=_cparams(("arbitrary", "arbitrary")),
        name="in_proj",
    )(x2, mod3, w_main_t, w_if_t, w_ift)


def _mlstm_kernel(qp_ref, kp_ref, q_ref, k_ref, v_ref, o_ref, g_ref, gt_ref, bgr_ref, bgc_ref,
                  cw_ref, ng_ref, y_ref, C_ref, n_ref, m_ref):
    c = pl.program_id(1)
    L = q_ref.shape[1]
    MW = q_ref.shape[2]
    dh = MW // M_HEADS
    H = M_HEADS

    @pl.when(c == 0)
    def _():
        C_ref[...] = jnp.zeros_like(C_ref)
        n_ref[...] = jnp.zeros_like(n_ref)
        m_ref[...] = jnp.zeros_like(m_ref)

    row = lax.broadcasted_iota(I32, (L, L), 0)
    col = lax.broadcasted_iota(I32, (L, L), 1)
    causal = col <= row
    tri = causal.astype(BF16)
    tri_t = (row <= col).astype(BF16)

    gb = g_ref[0] + bgr_ref[...]
    gtb = gt_ref[...] + bgc_ref[:, 0:1]
    b_col_all = sum(jnp.dot(tri, p, preferred_element_type=F32) for p in _split3(_log_sigmoid(gb)))
    b_row_all = sum(jnp.dot(p, tri_t, preferred_element_type=F32) for p in _split3(_log_sigmoid(gtb)))

    has_prev = c > 0

    for h in range(H):
        hs = slice(h * dh, (h + 1) * dh)
        ig_col = gb[:, h:h + 1]
        b_col = b_col_all[:, H + h:H + h + 1]
        ig_row = gtb[h:h + 1, :]
        b_row = b_row_all[H + h:H + h + 1, :]
        b_last = b_row[:, L - 1:L]
        m_prev = m_ref[h][0:1, 0:1]

        qprev = jnp.where(has_prev, qp_ref[0, :, hs], 0.0)
        kprev = jnp.where(has_prev, kp_ref[0, :, hs], 0.0)
        q = _silu(_causal_conv(q_ref[0, :, hs], qprev, cw_ref[:, hs]))
        k = _silu(_causal_conv(k_ref[0, :, hs], kprev,
                               cw_ref[:, MW + h * dh:MW + (h + 1) * dh])) * (dh ** -0.5)
        qb = q.astype(BF16)
        kb = k.astype(BF16)
        vb = v_ref[0, :, hs].astype(BF16)

        qk = lax.dot_general(qb, kb, (((1,), (1,)), ((), ())), preferred_element_type=F32)
        dmat = jnp.where(causal, b_col - b_row + ig_row, -jnp.inf)
        inter = b_col + m_prev
        m_t = jnp.maximum(inter, jnp.max(dmat, axis=-1, keepdims=True))
        s = qk * jnp.exp(dmat - m_t)
        e_inter = jnp.exp(inter - m_t)
        C_old = C_ref[h]
        n_old = n_ref[h]
        num = (jnp.dot(s.astype(BF16), vb, preferred_element_type=F32)
               + e_inter * jnp.dot(qb, C_old.astype(BF16), preferred_element_type=F32))
        den = (jnp.sum(s, axis=-1, keepdims=True)
               + e_inter * jnp.sum(q * n_old, axis=-1, keepdims=True))
        hval = num / jnp.maximum(jnp.abs(den), jnp.exp(-m_t))

        g_col = b_last - b_col + ig_col
        g_row = b_last - b_row + ig_row
        m_new = jnp.maximum(b_last + m_prev, jnp.max(g_row, axis=-1, keepdims=True))
        wk = jnp.exp(g_col - m_new)
        decay = jnp.exp(b_last + m_prev - m_new)
        kw = k * wk
        C_ref[h] = decay * C_old + lax.dot_general(kw.astype(BF16), vb, (((0,), (0,)), ((), ())),
                                                   preferred_element_type=F32)
        n_ref[h] = decay * n_old + jnp.sum(kw, axis=0, keepdims=True)
        m_ref[h] = jnp.broadcast_to(m_new, m_ref.shape[1:])

        hnorm = hval * lax.rsqrt(jnp.mean(hval * hval, axis=-1, keepdims=True) + EPS)
        ym = hnorm * ng_ref[:, hs] * jax.nn.sigmoid(o_ref[0, :, hs])
        y_ref[0, :, hs] = ym.astype(BF16)


def _mlstm(proj3, gates3, gates_t, bg_row, bg_col, conv_qk, mh_norm_g, L):
    B, S, _ = proj3.shape
    MW = mh_norm_g.shape[-1]
    dh = MW // M_HEADS
    nc = S // L
    l8 = L // SUBLANES

    def prev_map(colblk):
        return lambda b, c: (b, jnp.maximum(c * l8 - 1, 0), colblk)

    def cur_map(colblk):
        return lambda b, c: (b, c, colblk)

    return pl.pallas_call(
        _mlstm_kernel,
        grid=(B, nc),
        in_specs=[pl.BlockSpec((1, SUBLANES, MW), prev_map(0)),
                  pl.BlockSpec((1, SUBLANES, MW), prev_map(1)),
                  pl.BlockSpec((1, L, MW), cur_map(0)),
                  pl.BlockSpec((1, L, MW), cur_map(1)),
                  pl.BlockSpec((1, L, MW), cur_map(2)),
                  pl.BlockSpec((1, L, MW), cur_map(3)),
                  pl.BlockSpec((1, L, LANES), lambda b, c: (b, c, 0)),
                  pl.BlockSpec((SUBLANES, L), lambda b, c: (0, b * nc + c)),
                  pl.BlockSpec((1, LANES), lambda b, c: (0, 0)),
                  pl.BlockSpec((SUBLANES, LANES), lambda b, c: (0, 0)),
                  pl.BlockSpec((CONV_WIDTH, 2 * MW), lambda b, c: (0, 0)),
                  pl.BlockSpec((1, MW), lambda b, c: (0, 0))],
        out_specs=pl.BlockSpec((1, L, MW), lambda b, c: (b, c, 0)),
        out_shape=jax.ShapeDtypeStruct((B, S, MW), BF16),
        scratch_shapes=[pltpu.VMEM((M_HEADS, dh, dh), F32),
                        pltpu.VMEM((M_HEADS, 1, dh), F32),
                        pltpu.VMEM((M_HEADS, SUBLANES, LANES), F32)],
        compiler_params=_cparams(("arbitrary", "arbitrary")),
        name="mlstm",
    )(proj3, proj3, proj3, proj3, proj3, proj3, gates3, gates_t, bg_row, bg_col, conv_qk,
      mh_norm_g)


def _lru_scan(a, u, h0):
    T, C = a.shape
    G = T // SUBLANES
    a = a.reshape(G, SUBLANES, C)
    u = u.reshape(G, SUBLANES, C)
    rid = lax.broadcasted_iota(I32, a.shape, 1)
    k = 1
    while k < SUBLANES:
        keep = rid >= k
        a_sh = jnp.where(keep, pltpu.roll(a, k, axis=1), 1.0)
        u_sh = jnp.where(keep, pltpu.roll(u, k, axis=1), 0.0)
        u = a * u_sh + u
        a = a * a_sh
        k *= 2
    h = h0
    groups = []
    for g in range(G):
        blk = u[g] + a[g] * h
        groups.append(blk)
        h = blk[SUBLANES - 1:SUBLANES, :]
    return jnp.concatenate(groups, axis=0)


def _rglru_kernel(xp_ref, x_ref, gr_ref, cw_ref, cb_ref, wax_ref, ba_ref, bx_ref, lam_ref, ng_ref,
                  y_ref, h_ref):
    t = pl.program_id(1)
    T = x_ref.shape[1]
    RW = x_ref.shape[2]
    bd = RW // R_BLOCKS

    @pl.when(t == 0)
    def _():
        h_ref[...] = jnp.zeros_like(h_ref)

    prev = jnp.where(t > 0, xp_ref[0], 0.0)
    xr = _causal_conv(x_ref[0], prev, cw_ref[...]) + cb_ref[...]
    xrb = xr.astype(BF16)
    ls = _log_sigmoid(lam_ref[...])
    for n in range(R_BLOCKS):
        sl = slice(n * bd, (n + 1) * bd)
        z = jnp.dot(xrb[:, sl], wax_ref[n], preferred_element_type=F32)
        r_gate = jax.nn.sigmoid(z[:, :bd] + ba_ref[:, sl])
        i_gate = jax.nn.sigmoid(z[:, bd:] + bx_ref[:, sl])
        log_a = LRU_C * r_gate * ls[:, sl]
        a = jnp.exp(log_a)
        u = jnp.sqrt(-jnp.tanh(log_a) * (a * a + 1.0)) * (i_gate * xr[:, sl])
        hseq = _lru_scan(a, u, h_ref[:, sl])
        h_ref[:, sl] = hseq[T - 1:T, :]
        y = hseq * jax.nn.gelu(gr_ref[0, :, sl])
        y = y * lax.rsqrt(jnp.mean(y * y, axis=-1, keepdims=True) + EPS) * ng_ref[:, sl]
        y_ref[0, :, sl] = y.astype(BF16)


def _rglru(proj3, lru_conv_w, lru_conv_b, wax, b_a, b_x, lam, ng, T, xr_blk, gr_blk):
    B, S, _ = proj3.shape
    RW = lam.shape[-1]
    bd = RW // R_BLOCKS
    t8 = T // SUBLANES
    vec = pl.BlockSpec((1, RW), lambda b, t: (0, 0))
    return pl.pallas_call(
        _rglru_kernel,
        grid=(B, S // T),
        in_specs=[pl.BlockSpec((1, SUBLANES, RW), lambda b, t: (b, jnp.maximum(t * t8 - 1, 0), xr_blk)),
                  pl.BlockSpec((1, T, RW), lambda b, t: (b, t, xr_blk)),
                  pl.BlockSpec((1, T, RW), lambda b, t: (b, t, gr_blk)),
                  pl.BlockSpec((CONV_WIDTH, RW), lambda b, t: (0, 0)),
                  vec,
                  pl.BlockSpec((R_BLOCKS, bd, 2 * bd), lambda b, t: (0, 0, 0)),
                  vec, vec, vec, vec],
        out_specs=pl.BlockSpec((1, T, RW), lambda b, t: (b, t, 0)),
        out_shape=jax.ShapeDtypeStruct((B, S, RW), BF16),
        scratch_shapes=[pltpu.VMEM((1, RW), F32)],
        compiler_params=_cparams(("arbitrary", "arbitrary")),
        name="rglru",
    )(proj3, proj3, proj3, lru_conv_w, lru_conv_b, wax, b_a, b_x, lam, ng)


def _rows_to_token_tiles(x, stage_ref, tiles_ref):
    T, D = x.shape
    nq = D // LANES
    for q in range(nq):
        stage_ref[pl.ds(q, T, stride=nq), :] = x[:, q * LANES:(q + 1) * LANES]

    def per_token(t, carry):
        tiles_ref[t] = stage_ref[pl.ds(pl.multiple_of(t * nq, nq), nq), :].astype(BF16)
        return carry

    lax.fori_loop(0, T, per_token, 0, unroll=8)


def _token_tiles_to_rows(tiles_ref, stage_ref, dtype):
    T, nq, _ = tiles_ref.shape

    def per_token(t, carry):
        stage_ref[pl.ds(pl.multiple_of(t * nq, nq), nq), :] = tiles_ref[t].astype(F32)
        return carry

    lax.fori_loop(0, T, per_token, 0, unroll=8)
    return jnp.concatenate([stage_ref[pl.ds(q, T, stride=nq), :].astype(dtype) for q in range(nq)],
                           axis=1)


def _outproj_kernel(x_ref, ym_ref, yr_ref, mod_ref, wm_ref, wr_ref, wrt_ref, brt_ref,
                    x1_ref, hp_ref, lt_ref, stage_ref):
    mix = (jnp.dot(ym_ref[...], wm_ref[...], preferred_element_type=F32)
           + jnp.dot(yr_ref[...], wr_ref[...], preferred_element_type=F32))
    x1 = x_ref[...] + mod_ref[0, 2:3, :] * mix
    x1_ref[...] = x1
    ms = jnp.mean(x1 * x1, axis=-1, keepdims=True)
    hn = x1 * lax.rsqrt(ms + EPS) * (1.0 + mod_ref[0, 4:5, :]) + mod_ref[0, 3:4, :]
    lt_ref[...] = lax.dot_general(wrt_ref[...], hn.astype(BF16), _NT,
                                  preferred_element_type=F32) + brt_ref[:, 0:1]
    _rows_to_token_tiles(hn, stage_ref, hp_ref)


def _out_proj(x2, ym2, yr2, mod3, w_out_b, w_rt_t, b_rt, S, tm):
    N, D = x2.shape
    MW = ym2.shape[1]
    RW = yr2.shape[1]
    assert MW == RW, "the two head groups share one row-block size of w_out"
    NR = w_rt_t.shape[0]
    spb = S // tm
    return pl.pallas_call(
        _outproj_kernel,
        grid=(N // tm,),
        in_specs=[pl.BlockSpec((tm, D), lambda i: (i, 0)),
                  pl.BlockSpec((tm, MW), lambda i: (i, 0)),
                  pl.BlockSpec((tm, RW), lambda i: (i, 0)),
                  pl.BlockSpec((1, 6, D), lambda i: (i // spb, 0, 0)),
                  pl.BlockSpec((MW, D), lambda i: (0, 0)),
                  pl.BlockSpec((RW, D), lambda i: (1, 0)),
                  pl.BlockSpec((NR, D), lambda i: (0, 0)),
                  pl.BlockSpec((NR, LANES), lambda i: (0, 0))],
        out_specs=[pl.BlockSpec((tm, D), lambda i: (i, 0)),
                   pl.BlockSpec((tm, D // LANES, LANES), lambda i: (i, 0, 0)),
                   pl.BlockSpec((NR, tm), lambda i: (0, i))],
        out_shape=[jax.ShapeDtypeStruct((N, D), F32),
                   jax.ShapeDtypeStruct((N, D // LANES, LANES), BF16),
                   jax.ShapeDtypeStruct((NR, N), F32)],
        scratch_shapes=[pltpu.VMEM((tm * (D // LANES), LANES), F32)],
        compiler_params=_cparams(("arbitrary",)),
        name="out_proj",
    )(x2, ym2, yr2, mod3, w_out_b, w_out_b, w_rt_t, b_rt)


def _router_kernel(lt_ref, rw_ref, meta_ref, d_ref, ri_scr, pcol_scr, carry_ref):
    phase = pl.program_id(0)
    i = pl.program_id(1)

    @pl.when(phase == 0)
    def _():
        _route_tile(i, lt_ref, ri_scr.at[i], rw_ref, meta_ref, pcol_scr, carry_ref)

    @pl.when(phase == 1)
    def _():
        _slots_tile(ri_scr.at[i], pcol_scr, d_ref)


def _route_tile(i, lt_ref, ri_ref, rw_ref, meta_ref, pcol_ref, carry_ref):
    tm = lt_ref.shape[1]
    E8 = EXPERTS_PER_GROUP

    @pl.when(i == 0)
    def _():
        carry_ref[...] = jnp.zeros_like(carry_ref)
        meta_ref[...] = jnp.zeros_like(meta_ref)
        pcol_ref[...] = jnp.zeros_like(pcol_ref)

    sub = lax.broadcasted_iota(I32, (SUBLANES, tm), 0)
    gl = jnp.where(sub < N_GROUPS, lt_ref[N_EXPERTS:N_EXPERTS + SUBLANES, :], -jnp.inf)
    ge = jnp.exp(gl - jnp.max(gl, axis=0, keepdims=True))
    pg = ge / jnp.sum(ge, axis=0, keepdims=True)
    pg_sel = jnp.max(pg, axis=0, keepdims=True)
    g_sel = jnp.min(jnp.where(pg == pg_sel, sub, SUBLANES), axis=0, keepdims=True)

    el = lt_ref[(N_GROUPS - 1) * E8:N_GROUPS * E8, :]
    for g in range(N_GROUPS - 2, -1, -1):
        el = jnp.where(g_sel == g, lt_ref[g * E8:(g + 1) * E8, :], el)
    ee = jnp.exp(el - jnp.max(el, axis=0, keepdims=True))
    pe = ee / jnp.sum(ee, axis=0, keepdims=True)
    p0 = jnp.max(pe, axis=0, keepdims=True)
    i0 = jnp.min(jnp.where(pe == p0, sub, SUBLANES), axis=0, keepdims=True)
    pe1 = jnp.where(sub == i0, -1.0, pe)
    p1 = jnp.max(pe1, axis=0, keepdims=True)
    i1 = jnp.min(jnp.where(pe1 == p1, sub, SUBLANES), axis=0, keepdims=True)
    psum = p0 + p1
    w0 = pg_sel * p0 / psum
    w1 = pg_sel * p1 / psum
    e0 = g_sel * E8 + i0
    e1 = g_sel * E8 + i1

    eid = lax.broadcasted_iota(I32, (N_EXPERTS, tm), 0)
    oh0 = (eid == e0).astype(F32)
    oh1 = (eid == e1).astype(F32)
    oh = oh0 + oh1
    r_ = lax.broadcasted_iota(I32, (tm, tm), 0)
    c_ = lax.broadcasted_iota(I32, (tm, tm), 1)
    before = (r_ < c_).astype(BF16)
    cnt = carry_ref[:, 0:1] + jnp.dot(oh.astype(BF16), before, preferred_element_type=F32)
    pos0 = jnp.sum(oh0 * cnt, axis=0, keepdims=True)
    pos1 = jnp.sum(oh1 * cnt, axis=0, keepdims=True)
    new_carry = carry_ref[...] + jnp.sum(oh, axis=1, keepdims=True)
    carry_ref[...] = new_carry

    ri = jnp.where(sub == 0, e0, 0)
    ri = jnp.where(sub == 1, e1, ri)
    ri = jnp.where(sub == 2, pos0.astype(I32), ri)
    ri = jnp.where(sub == 3, pos1.astype(I32), ri)
    ri_ref[...] = ri
    rw_ref[...] = jnp.where(sub == 0, w0, jnp.where(sub == 1, w1, 0.0))

    @pl.when(i == pl.num_programs(1) - 1)
    def _():
        nbp = meta_ref.shape[1]
        cnt_col = new_carry[:, 0:1]
        padded_col = jnp.ceil(cnt_col / MOE_ROWS) * MOE_ROWS
        nonempty = cnt_col > 0.0
        e_sub = lax.broadcasted_iota(I32, (N_EXPERTS, nbp), 0)
        lane = lax.broadcasted_iota(I32, (N_EXPERTS, nbp), 1)
        padded_row = jnp.sum(jnp.where(e_sub == lane, padded_col, 0.0), axis=0, keepdims=True)
        pend_row = jnp.sum(jnp.where(e_sub <= lane, padded_col, 0.0), axis=0, keepdims=True)
        pend_col = jnp.sum(jnp.where(lane <= e_sub, padded_row, 0.0), axis=1, keepdims=True)
        blk_start = lane.astype(F32) * MOE_ROWS
        e_f = e_sub.astype(F32)
        be = jnp.sum(jnp.where(pend_col <= blk_start, 1.0, 0.0), axis=0, keepdims=True)
        be = jnp.minimum(be, N_EXPERTS - 1.0)
        nxt = jnp.min(jnp.where(jnp.logical_and(e_f > be, nonempty), e_f, float(N_EXPERTS)),
                      axis=0, keepdims=True)
        run = jnp.sum(jnp.where(jnp.logical_and(e_f < be, nonempty), 1.0, 0.0),
                      axis=0, keepdims=True)
        cnt_row = jnp.sum(jnp.where(e_sub == lane, cnt_col, 0.0), axis=0, keepdims=True)
        n_used = pend_row[:, N_EXPERTS - 1:N_EXPERTS] / MOE_ROWS
        sub8 = lax.broadcasted_iota(I32, (SUBLANES, nbp), 0)
        meta = jnp.where(sub8 == META_BLOCK_E, be, 0.0)
        meta = jnp.where(sub8 == META_NEXT_E, nxt, meta)
        meta = jnp.where(sub8 == META_SLOT, run - 2.0 * jnp.floor(run * 0.5), meta)
        meta = jnp.where(sub8 == META_NUSED, n_used, meta)
        meta = jnp.where(sub8 == META_PAD_LO, pend_row - padded_row + cnt_row, meta)
        meta = jnp.where(sub8 == META_PAD_N, padded_row - cnt_row, meta)
        meta_ref[...] = meta.astype(I32)
        pcol_ref[...] = jnp.broadcast_to(pend_col - padded_col, pcol_ref.shape)


def _router(logits_t, tm, n_blocks):
    NR, N = logits_t.shape
    nt = N // tm
    nbp = max(-(-n_blocks // LANES), 1) * LANES
    routed = lambda p, i: (0, jnp.where(p == 0, i, nt - 1))
    return pl.pallas_call(
        _router_kernel,
        grid=(2, nt),
        in_specs=[pl.BlockSpec((NR, tm), routed)],
        out_specs=[pl.BlockSpec((SUBLANES, tm), routed),
                   pl.BlockSpec((SUBLANES, nbp), lambda p, i: (0, 0)),
                   pl.BlockSpec((SUBLANES, tm), lambda p, i: (0, jnp.where(p == 0, 0, i)))],
        out_shape=[jax.ShapeDtypeStruct((SUBLANES, N), F32),
                   jax.ShapeDtypeStruct((SUBLANES, nbp), I32),
                   jax.ShapeDtypeStruct((SUBLANES, N), I32)],
        scratch_shapes=[pltpu.VMEM((nt, SUBLANES, tm), I32),
                        pltpu.VMEM((N_EXPERTS, LANES), F32),
                        pltpu.VMEM((N_EXPERTS, LANES), F32)],
        compiler_params=_cparams(("arbitrary", "arbitrary")),
        name="router",
    )(logits_t)


def _slots_tile(ri_ref, pcol_ref, d_ref):
    tm = ri_ref.shape[1]
    eid = lax.broadcasted_iota(I32, (N_EXPERTS, tm), 0)
    pstart = pcol_ref[:, 0:1]
    sub = lax.broadcasted_iota(I32, (SUBLANES, tm), 0)
    out = jnp.zeros((SUBLANES, tm), I32)
    for kk in range(TOP_K):
        first = jnp.sum(jnp.where(eid == ri_ref[kk:kk + 1, :], pstart, 0.0), axis=0, keepdims=True)
        out = jnp.where(sub == kk, first.astype(I32) + ri_ref[TOP_K + kk:TOP_K + kk + 1, :], out)
    d_ref[...] = out


ROW_DMA_UNROLL = 8


def _row_copy(src, dst, sem):
    return pltpu.make_async_copy(src, dst, sem)


def _unrolled(n, fn):
    def trip(g, carry):
        for u in range(ROW_DMA_UNROLL):
            fn(g * ROW_DMA_UNROLL + u)
        return carry
    lax.fori_loop(0, n // ROW_DMA_UNROLL, trip, 0)


def _dispatch_kernel(meta_ref, dest_ref, hp_ref, hp_hbm, xs_ref, zero_ref, sem):
    i = pl.program_id(0)
    tm = hp_ref.shape[0]
    n_blocks = xs_ref.shape[0] // MOE_ROWS

    def issue(t):
        _row_copy(hp_ref.at[t], xs_ref.at[dest_ref[0, t]], sem.at[0]).start(
            priority=0)
        _row_copy(hp_hbm.at[i * tm + t], xs_ref.at[dest_ref[1, t]], sem.at[2]).start()

    def issue_odd(t):
        _row_copy(hp_ref.at[t], xs_ref.at[dest_ref[0, t]], sem.at[0]).start(
            priority=1)
        _row_copy(hp_hbm.at[i * tm + t], xs_ref.at[dest_ref[1, t]], sem.at[2]).start()

    def pair(g):
        issue(2 * g)
        issue_odd(2 * g + 1)

    _unrolled(tm // 2, pair)
    _unrolled(tm, lambda t: _row_copy(hp_ref.at[0], xs_ref.at[0], sem.at[0]).wait())
    _unrolled(tm, lambda t: _row_copy(hp_hbm.at[0], xs_ref.at[0], sem.at[2]).wait())

    @pl.when(i == pl.num_programs(0) - 1)
    def _():
        zero_ref[...] = jnp.zeros_like(zero_ref)
        for e in range(N_EXPERTS):
            lo = meta_ref[META_PAD_LO, e]
            n_pad = meta_ref[META_PAD_N, e]

            def fill(r, carry):
                _row_copy(zero_ref.at[0], xs_ref.at[lo + r], sem.at[1]).start()
                return carry

            def drain(r, carry):
                _row_copy(zero_ref.at[0], xs_ref.at[0], sem.at[1]).wait()
                return carry

            lax.fori_loop(0, n_pad, fill, 0)
            lax.fori_loop(0, n_pad, drain, 0)

        def block_copy(b):
            return _row_copy(zero_ref, xs_ref.at[pl.ds(b * MOE_ROWS, MOE_ROWS)], sem.at[1])

        n_used = meta_ref[META_NUSED, 0]
        lax.fori_loop(n_used, n_blocks, lambda b, c: (block_copy(b).start(), c)[1], 0)
        lax.fori_loop(n_used, n_blocks, lambda b, c: (block_copy(0).wait(), c)[1], 0)


def _dispatch(meta, dest, hp, n_slots, tm):
    N, nq, _ = hp.shape
    grid_spec = pltpu.PrefetchScalarGridSpec(
        num_scalar_prefetch=1,
        grid=(N // tm,),
        in_specs=[pl.BlockSpec((SUBLANES, tm), lambda i, m: (0, i), memory_space=pltpu.SMEM),
                  pl.BlockSpec((tm, nq, LANES), lambda i, m: (i, 0, 0)),
                  pl.BlockSpec(memory_space=pl.ANY)],
        out_specs=pl.BlockSpec(memory_space=pl.ANY),
        scratch_shapes=[pltpu.VMEM((MOE_ROWS, nq, LANES), BF16), pltpu.SemaphoreType.DMA((3,))],
    )
    return pl.pallas_call(
        _dispatch_kernel,
        grid_spec=grid_spec,
        out_shape=jax.ShapeDtypeStruct((n_slots, nq, LANES), BF16),
        compiler_params=_cparams(("arbitrary",)),
        name="dispatch",
    )(meta, dest, hp, hp)


def _expert_kernel(meta_ref, xs_ref, wg_hbm, wu_hbm, wd_hbm, ys_ref, wbuf_g, wbuf_u, wbuf_d,
                   wgu_s, wd_s, stage_ref, sem):
    i = pl.program_id(0)
    DE = wbuf_g.shape[2]
    e = meta_ref[META_BLOCK_E, i]
    e_prev = meta_ref[META_BLOCK_E, jnp.maximum(i - 1, 0)]
    used = i < meta_ref[META_NUSED, 0]
    first = jnp.logical_and(used, jnp.logical_or(i == 0, e != e_prev))
    slot = meta_ref[META_SLOT, i]
    nxt = meta_ref[META_NEXT_E, i]

    def weight_copies(ex, s):
        return (_row_copy(wg_hbm.at[ex], wbuf_g.at[s], sem.at[s]),
                _row_copy(wu_hbm.at[ex], wbuf_u.at[s], sem.at[s]),
                _row_copy(wd_hbm.at[ex], wbuf_d.at[s], sem.at[s]))

    @pl.when(jnp.logical_and(used, i == 0))
    def _():
        for cp in weight_copies(e, slot):
            cp.start()

    @pl.when(first)
    def _():
        @pl.when(nxt < N_EXPERTS)
        def _():
            for cp in weight_copies(nxt, 1 - slot):
                cp.start()

        for cp in weight_copies(e, slot):
            cp.wait()
        wgu_s[:, :DE] = wbuf_g[slot].astype(BF16)
        wgu_s[:, DE:] = wbuf_u[slot].astype(BF16)
        wd_s[...] = wbuf_d[slot].astype(BF16)

    @pl.when(used)
    def _():
        xb = _token_tiles_to_rows(xs_ref, stage_ref, BF16)
        gu = jnp.dot(xb, wgu_s[...], preferred_element_type=F32)
        hb = _silu(gu[:, :DE]) * gu[:, DE:]
        ys_ref[...] = jnp.dot(hb.astype(BF16), wd_s[...], preferred_element_type=F32)

    @pl.when(jnp.logical_not(used))
    def _():
        ys_ref[...] = jnp.zeros_like(ys_ref)


def _experts(meta, xs, w_e_gate, w_e_up, w_e_down):
    n_slots, nq, _ = xs.shape
    _, D, DE = w_e_gate.shape
    n_blocks = n_slots // MOE_ROWS
    hbm = pl.BlockSpec(memory_space=pl.ANY)
    grid_spec = pltpu.PrefetchScalarGridSpec(
        num_scalar_prefetch=1,
        grid=(n_blocks,),
        in_specs=[pl.BlockSpec((MOE_ROWS, nq, LANES), lambda i, m: (i, 0, 0)), hbm, hbm, hbm],
        out_specs=pl.BlockSpec((MOE_ROWS, D), lambda i, m: (i, 0)),
        scratch_shapes=[pltpu.VMEM((2, D, DE), F32), pltpu.VMEM((2, D, DE), F32),
                        pltpu.VMEM((2, DE, D), F32),
                        pltpu.VMEM((D, 2 * DE), BF16), pltpu.VMEM((DE, D), BF16),
                        pltpu.VMEM((MOE_ROWS * nq, LANES), F32),
                        pltpu.SemaphoreType.DMA((2,))],
    )
    return pl.pallas_call(
        _expert_kernel,
        grid_spec=grid_spec,
        out_shape=jax.ShapeDtypeStruct((n_slots, D), F32),
        compiler_params=_cparams(("arbitrary",)),
        name="experts",
    )(meta, xs, w_e_gate, w_e_up, w_e_down)


def _combine_kernel(rcur_ref, rnext_ref, x1_ref, wt_ref, mod_ref, fg_ref, ys_ref, o_ref, gbuf, sem):
    i = pl.program_id(0)
    n = pl.num_programs(0)
    tm = x1_ref.shape[0]
    slot = i % 2

    def gather(dest_ref, s):
        def issue(t):
            for kk in range(TOP_K):
                _row_copy(ys_ref.at[pl.ds(dest_ref[kk, t], 1)], gbuf.at[s, kk, pl.ds(t, 1)],
                          sem.at[s]).start(priority=kk)
        _unrolled(tm, issue)

    @pl.when(i == 0)
    def _():
        gather(rcur_ref, 0)

    @pl.when(i + 1 < n)
    def _():
        gather(rnext_ref, 1 - slot)

    _unrolled(TOP_K * tm,
              lambda t: _row_copy(ys_ref.at[pl.ds(0, 1)], gbuf.at[slot, 0, pl.ds(0, 1)],
                                  sem.at[slot]).wait())

    wt = wt_ref[...]
    y = wt[:, 0:1] * gbuf[slot, 0] + wt[:, 1:2] * gbuf[slot, 1]
    x2 = x1_ref[...] + mod_ref[0, 5:6, :] * y
    ms = jnp.mean(x2 * x2, axis=-1, keepdims=True)
    o_ref[...] = x2 * lax.rsqrt(ms + EPS) * fg_ref[...]


def _combine(dest, x1, wts, mod3, final_g, ys, S, tm):
    N, D = x1.shape
    spb = S // tm
    nsteps = N // tm
    return pl.pallas_call(
        _combine_kernel,
        grid=(nsteps,),
        in_specs=[pl.BlockSpec((SUBLANES, tm), lambda i: (0, i), memory_space=pltpu.SMEM),
                  pl.BlockSpec((SUBLANES, tm), lambda i: (0, jnp.minimum(i + 1, nsteps - 1)),
                               memory_space=pltpu.SMEM),
                  pl.BlockSpec((tm, D), lambda i: (i, 0)),
                  pl.BlockSpec((tm, SUBLANES), lambda i: (i, 0)),
                  pl.BlockSpec((1, 6, D), lambda i: (i // spb, 0, 0)),
                  pl.BlockSpec((1, D), lambda i: (0, 0)),
                  pl.BlockSpec(memory_space=pl.ANY)],
        out_specs=pl.BlockSpec((tm, D), lambda i: (i, 0)),
        out_shape=jax.ShapeDtypeStruct((N, D), F32),
        scratch_shapes=[pltpu.VMEM((2, TOP_K, tm, D), F32), pltpu.SemaphoreType.DMA((2,))],
        compiler_params=_cparams(("arbitrary",)),
        name="combine",
    )(dest, dest, x1, wts, mod3, final_g.reshape(1, D), ys)


def _tiles(S):
    def fit(t):
        return min(t, S)
    return dict(prep_tr=512, inproj_tm=fit(1024), inproj_tn=1024, chunk=fit(256), lru_t=fit(256),
                outproj_tm=fit(512), router_tm=fit(512), dispatch_tm=fit(256), combine_tm=fit(256))


def _layer(x, mod, w_in, b_gates, conv_qk, mh_norm_g, lru_conv_w, lru_conv_b, w_lru_a, b_lru_a,
           w_lru_x, b_lru_x, lru_lambda, lru_norm_g, w_out, w_group, b_group, w_router, b_router,
           w_e_gate, w_e_up, w_e_down, final_g):
    B, S, D = x.shape
    N = B * S
    MW = mh_norm_g.shape[-1]
    RW = lru_lambda.shape[-1]
    H = M_HEADS
    tl = _tiles(S)
    x2 = x.reshape(N, D)
    mod3 = mod.reshape(B, 6, D)

    n_qkvo = 4 * MW
    w_in_t = w_in.T
    w_main_t = _prep_w_in(w_in_t, n_qkvo, tl["prep_tr"])
    w_gate_t = w_in_t[n_qkvo:n_qkvo + 2 * H]
    w_ift = jnp.zeros((SUBLANES, D), F32).at[:2 * H].set(w_gate_t).astype(BF16)
    w_if_t = jnp.zeros((LANES, D), F32).at[:2 * H].set(w_gate_t).astype(BF16)
    proj, gates, gates_t = _in_proj(x2, mod3, w_main_t, w_if_t, w_ift, S, tl["inproj_tm"],
                                    tl["inproj_tn"])
    proj3 = proj.reshape(B, S, -1)

    bg_row = jnp.zeros((1, LANES), F32).at[0, :2 * H].set(b_gates)
    bg_col = jnp.zeros((SUBLANES, LANES), F32).at[:2 * H, :].set(b_gates[:, None])
    ym = _mlstm(proj3, gates.reshape(B, S, LANES), gates_t, bg_row, bg_col, conv_qk,
                mh_norm_g.reshape(1, MW), tl["chunk"])

    wax = jnp.concatenate([w_lru_a, w_lru_x], axis=-1).astype(BF16)
    yr = _rglru(proj3, lru_conv_w, lru_conv_b.reshape(1, RW), wax, b_lru_a.reshape(1, RW),
                b_lru_x.reshape(1, RW), lru_lambda.reshape(1, RW), lru_norm_g.reshape(1, RW),
                tl["lru_t"], n_qkvo // RW, n_qkvo // RW + 1)

    NR = N_EXPERTS + SUBLANES
    w_rt_t = (jnp.zeros((NR, D), F32).at[:N_EXPERTS].set(w_router.T)
              .at[N_EXPERTS:N_EXPERTS + N_GROUPS].set(w_group.T)).astype(BF16)
    b_rt = (jnp.zeros((NR, LANES), F32).at[:N_EXPERTS, :].set(b_router[:, None])
            .at[N_EXPERTS:N_EXPERTS + N_GROUPS, :].set(b_group[:, None]))
    x1, hp, logits_t = _out_proj(x2, ym.reshape(N, MW), yr.reshape(N, RW), mod3,
                                 w_out.astype(BF16), w_rt_t, b_rt, S, tl["outproj_tm"])

    n_slots = N * TOP_K + N_EXPERTS * MOE_ROWS
    rw, meta, dest = _router(logits_t, tl["router_tm"], n_slots // MOE_ROWS)
    xs = _dispatch(meta, dest, hp, n_slots, tl["dispatch_tm"])
    ys = _experts(meta, xs, w_e_gate, w_e_up, w_e_down)
    out = _combine(dest, x1, rw.T, mod3, final_g, ys, S, tl["combine_tm"])
    return out.reshape(B, S, D)


def kernel(x, c, w_ada, b_ada, w_in, b_gates, conv_qk, mh_norm_g, lru_conv_w, lru_conv_b, w_lru_a, b_lru_a, w_lru_x, b_lru_x, lru_lambda, lru_norm_g, w_out, w_group, b_group, w_router, b_router, w_e_gate, w_e_up, w_e_down, final_g):
    depth = w_ada.shape[0]
    assert depth == 1, "single trunk layer"
    l = 0
    mod = _ada_mod(c, w_ada[l], b_ada[l])
    return _layer(x, mod, w_in[l], b_gates[l], conv_qk[l], mh_norm_g[l], lru_conv_w[l],
                  lru_conv_b[l], w_lru_a[l], b_lru_a[l], w_lru_x[l], b_lru_x[l], lru_lambda[l],
                  lru_norm_g[l], w_out[l], w_group[l], b_group[l], w_router[l], b_router[l],
                  w_e_gate[l], w_e_up[l], w_e_down[l], final_g)
```

```python
import functools

import jax
import jax.numpy as jnp
from jax import lax
from jax.experimental import pallas as pl
from jax.experimental.pallas import tpu as pltpu

F32 = jnp.float32
BF16 = jnp.bfloat16
U32 = jnp.uint32
I32 = jnp.int32

EPS = 1e-6
M_HEADS = 4
R_BLOCKS = 8
CONV_WIDTH = 4
LRU_C = 8.0
N_GROUPS = 4
EXPERTS_PER_GROUP = 8
N_EXPERTS = N_GROUPS * EXPERTS_PER_GROUP
TOP_K = 2

LANES = 128
SUBLANES = 8
VMEM_LIMIT = 56 * 1024 * 1024
PROLOGUE_ROWS = 256

MOE_ROWS = 256
META_BLOCK_E, META_NEXT_E, META_SLOT, META_NUSED, META_PAD_LO, META_PAD_N = range(6)


def _cparams(sem, vmem=VMEM_LIMIT):
    return pltpu.CompilerParams(dimension_semantics=sem, vmem_limit_bytes=vmem)


def _log_sigmoid(x):
    return jnp.minimum(x, 0.0) - jnp.log1p(jnp.exp(-jnp.abs(x)))


def _silu(x):
    return x * jax.nn.sigmoid(x)


def _split3(x):
    hi = x.astype(BF16)
    r1 = x - hi.astype(F32)
    mid = r1.astype(BF16)
    lo = (r1 - mid.astype(F32)).astype(BF16)
    return hi, mid, lo


def _causal_conv(cur, prev8, w4):
    T, C = cur.shape
    G = T // SUBLANES
    x3 = cur.reshape(G, SUBLANES, C)
    p3 = prev8.reshape(1, SUBLANES, C)
    rid = lax.broadcasted_iota(I32, x3.shape, 1)
    acc = x3 * w4[CONV_WIDTH - 1:CONV_WIDTH, :]
    for d in range(1, CONV_WIDTH):
        rot = pltpu.roll(x3, d, axis=1)
        before = jnp.concatenate([pltpu.roll(p3, d, axis=1), rot[:G - 1]], axis=0)
        acc = acc + jnp.where(rid < d, before, rot) * w4[CONV_WIDTH - 1 - d:CONV_WIDTH - d, :]
    return acc.reshape(T, C)


def _ada_kernel(c_ref, w_ref, b_ref, o_ref):
    s = _silu(c_ref[...])
    o_ref[...] = jnp.dot(s.astype(BF16), w_ref[...].astype(BF16),
                         preferred_element_type=F32) + b_ref[...]


def _ada_mod(c, w_ada, b_ada):
    B, D = c.shape
    n6 = w_ada.shape[1]
    tn = 1024
    cp = jnp.zeros((SUBLANES, D), F32).at[:B].set(c)
    out = pl.pallas_call(
        _ada_kernel,
        grid=(n6 // tn,),
        in_specs=[pl.BlockSpec((SUBLANES, D), lambda j: (0, 0)),
                  pl.BlockSpec((D, tn), lambda j: (0, j)),
                  pl.BlockSpec((1, tn), lambda j: (0, j))],
        out_specs=pl.BlockSpec((SUBLANES, tn), lambda j: (0, j)),
        out_shape=jax.ShapeDtypeStruct((SUBLANES, n6), F32),
        compiler_params=_cparams(("arbitrary",)),
        name="ada_mod",
    )(cp, w_ada, b_ada.reshape(1, n6))
    return out[:B]


_NT = (((1,), (1,)), ((), ()))


def _prep_kernel(w_ref, wm_ref):
    wm_ref[...] = w_ref[...].astype(BF16)


def _prep_w_in(w_in_t, n_qkvo, tr):
    nin, D = w_in_t.shape
    n_gate = 2 * M_HEADS
    nm = nin - n_gate
    assert n_qkvo % tr == 0 and nm % tr == 0 and n_gate % SUBLANES == 0
    return pl.pallas_call(
        _prep_kernel,
        grid=(nm // tr,),
        in_specs=[pl.BlockSpec((pl.Element(tr), pl.Element(D)),
                               lambda i: (pl.multiple_of(
                                   i * tr + jnp.where(i * tr < n_qkvo, 0, n_gate), SUBLANES), 0))],
        out_specs=pl.BlockSpec((tr, D), lambda i: (i, 0)),
        out_shape=jax.ShapeDtypeStruct((nm, D), BF16),
        compiler_params=_cparams(("arbitrary",)),
        name="prep_w_in",
    )(w_in_t)


def _inproj_kernel(x_ref, mod_ref, w_ref, wif_ref, wift_ref, o_ref, g_ref, gt_ref, hn_ref):
    j = pl.program_id(1)

    @pl.when(j == 0)
    def _():
        tm = x_ref.shape[0]
        rc = min(tm, PROLOGUE_ROWS)
        for r in range(tm // rc):
            rows = slice(r * rc, (r + 1) * rc)
            x = x_ref[rows, :]
            ms = jnp.mean(x * x, axis=-1, keepdims=True)
            hn = x * lax.rsqrt(ms + EPS) * (1.0 + mod_ref[0, 1:2, :]) + mod_ref[0, 0:1, :]
            hb = hn.astype(BF16)
            hn_ref[rows, :] = hb
            g_ref[rows, :] = lax.dot_general(hb, wif_ref[...], _NT, preferred_element_type=F32)
            gt_ref[:, rows] = lax.dot_general(wift_ref[...], hb, _NT, preferred_element_type=F32)

    o_ref[...] = lax.dot_general(hn_ref[...], w_ref[...], _NT, preferred_element_type=F32)


def _in_proj(x2, mod3, w_main_t, w_if_t, w_ift, S, tm, tn):
    N, D = x2.shape
    nw = w_main_t.shape[0]
    spb = S // tm
    return pl.pallas_call(
        _inproj_kernel,
        grid=(N // tm, nw // tn),
        in_specs=[pl.BlockSpec((tm, D), lambda i, j: (i, 0)),
                  pl.BlockSpec((1, 6, D), lambda i, j: (i // spb, 0, 0)),
                  pl.BlockSpec((tn, D), lambda i, j: (j, 0)),
                  pl.BlockSpec((LANES, D), lambda i, j: (0, 0)),
                  pl.BlockSpec((SUBLANES, D), lambda i, j: (0, 0))],
        out_specs=[pl.BlockSpec((tm, tn), lambda i, j: (i, j)),
                   pl.BlockSpec((tm, LANES), lambda i, j: (i, 0)),
                   pl.BlockSpec((SUBLANES, tm), lambda i, j: (0, i))],
        out_shape=[jax.ShapeDtypeStruct((N, nw), F32),
                   jax.ShapeDtypeStruct((N, LANES), F32),
                   jax.ShapeDtypeStruct((SUBLANES, N), F32)],
        scratch_shapes=[pltpu.VMEM((tm, D), BF16)],
        compiler_params=_cparams(("arbitrary", "arbitrary")),
        name="in_proj",
    )(x2, mod3, w_main_t, w_if_t, w_ift)


def _mlstm_kernel(qp_ref, kp_ref, q_ref, k_ref, v_ref, o_ref, g_ref, gt_ref, bgr_ref, bgc_ref,
                  cw_ref, ng_ref, y_ref, C_ref, n_ref, m_ref):
    c = pl.program_id(1)
    L = q_ref.shape[1]
    MW = q_ref.shape[2]
    dh = MW // M_HEADS
    H = M_HEADS

    @pl.when(c == 0)
    def _():
        C_ref[...] = jnp.zeros_like(C_ref)
        n_ref[...] = jnp.zeros_like(n_ref)
        m_ref[...] = jnp.zeros_like(m_ref)

    row = lax.broadcasted_iota(I32, (L, L), 0)
    col = lax.broadcasted_iota(I32, (L, L), 1)
    causal = col <= row
    tri = causal.astype(BF16)
    tri_t = (row <= col).astype(BF16)

    gb = g_ref[0] + bgr_ref[...]
    gtb = gt_ref[...] + bgc_ref[:, 0:1]
    b_col_all = sum(jnp.dot(tri, p, preferred_element_type=F32) for p in _split3(_log_sigmoid(gb)))
    b_row_all = sum(jnp.dot(p, tri_t, preferred_element_type=F32) for p in _split3(_log_sigmoid(gtb)))

    has_prev = c > 0

    for h in range(H):
        hs = slice(h * dh, (h + 1) * dh)
        ig_col = gb[:, h:h + 1]
        b_col = b_col_all[:, H + h:H + h + 1]
        ig_row = gtb[h:h + 1, :]
        b_row = b_row_all[H + h:H + h + 1, :]
        b_last = b_row[:, L - 1:L]
        m_prev = m_ref[h][0:1, 0:1]

        qprev = jnp.where(has_prev, qp_ref[0, :, hs], 0.0)
        kprev = jnp.where(has_prev, kp_ref[0, :, hs], 0.0)
        q = _silu(_causal_conv(q_ref[0, :, hs], qprev, cw_ref[:, hs]))
        k = _silu(_causal_conv(k_ref[0, :, hs], kprev,
                               cw_ref[:, MW + h * dh:MW + (h + 1) * dh])) * (dh ** -0.5)
        qb = q.astype(BF16)
        kb = k.astype(BF16)
        vb = v_ref[0, :, hs].astype(BF16)

        qk = lax.dot_general(qb, kb, (((1,), (1,)), ((), ())), preferred_element_type=F32)
        dmat = jnp.where(causal, b_col - b_row + ig_row, -jnp.inf)
        inter = b_col + m_prev
        m_t = jnp.maximum(inter, jnp.max(dmat, axis=-1, keepdims=True))
        s = qk * jnp.exp(dmat - m_t)
        e_inter = jnp.exp(inter - m_t)
        C_old = C_ref[h]
        n_old = n_ref[h]
        num = (jnp.dot(s.astype(BF16), vb, preferred_element_type=F32)
               + e_inter * jnp.dot(qb, C_old.astype(BF16), preferred_element_type=F32))
        den = (jnp.sum(s, axis=-1, keepdims=True)
               + e_inter * jnp.sum(q * n_old, axis=-1, keepdims=True))
        hval = num / jnp.maximum(jnp.abs(den), jnp.exp(-m_t))

        g_col = b_last - b_col + ig_col
        g_row = b_last - b_row + ig_row
        m_new = jnp.maximum(b_last + m_prev, jnp.max(g_row, axis=-1, keepdims=True))
        wk = jnp.exp(g_col - m_new)
        decay = jnp.exp(b_last + m_prev - m_new)
        kw = k * wk
        C_ref[h] = decay * C_old + lax.dot_general(kw.astype(BF16), vb, (((0,), (0,)), ((), ())),
                                                   preferred_element_type=F32)
        n_ref[h] = decay * n_old + jnp.sum(kw, axis=0, keepdims=True)
        m_ref[h] = jnp.broadcast_to(m_new, m_ref.shape[1:])

        hnorm = hval * lax.rsqrt(jnp.mean(hval * hval, axis=-1, keepdims=True) + EPS)
        ym = hnorm * ng_ref[:, hs] * jax.nn.sigmoid(o_ref[0, :, hs])
        y_ref[0, :, hs] = ym.astype(BF16)


def _mlstm(proj3, gates3, gates_t, bg_row, bg_col, conv_qk, mh_norm_g, L):
    B, S, _ = proj3.shape
    MW = mh_norm_g.shape[-1]
    dh = MW // M_HEADS
    nc = S // L
    l8 = L // SUBLANES

    def prev_map(colblk):
        return lambda b, c: (b, jnp.maximum(c * l8 - 1, 0), colblk)

    def cur_map(colblk):
        return lambda b, c: (b, c, colblk)

    return pl.pallas_call(
        _mlstm_kernel,
        grid=(B, nc),
        in_specs=[pl.BlockSpec((1, SUBLANES, MW), prev_map(0)),
                  pl.BlockSpec((1, SUBLANES, MW), prev_map(1)),
                  pl.BlockSpec((1, L, MW), cur_map(0)),
                  pl.BlockSpec((1, L, MW), cur_map(1)),
                  pl.BlockSpec((1, L, MW), cur_map(2)),
                  pl.BlockSpec((1, L, MW), cur_map(3)),
                  pl.BlockSpec((1, L, LANES), lambda b, c: (b, c, 0)),
                  pl.BlockSpec((SUBLANES, L), lambda b, c: (0, b * nc + c)),
                  pl.BlockSpec((1, LANES), lambda b, c: (0, 0)),
                  pl.BlockSpec((SUBLANES, LANES), lambda b, c: (0, 0)),
                  pl.BlockSpec((CONV_WIDTH, 2 * MW), lambda b, c: (0, 0)),
                  pl.BlockSpec((1, MW), lambda b, c: (0, 0))],
        out_specs=pl.BlockSpec((1, L, MW), lambda b, c: (b, c, 0)),
        out_shape=jax.ShapeDtypeStruct((B, S, MW), BF16),
        scratch_shapes=[pltpu.VMEM((M_HEADS, dh, dh), F32),
                        pltpu.VMEM((M_HEADS, 1, dh), F32),
                        pltpu.VMEM((M_HEADS, SUBLANES, LANES), F32)],
        compiler_params=_cparams(("arbitrary", "arbitrary")),
        name="mlstm",
    )(proj3, proj3, proj3, proj3, proj3, proj3, gates3, gates_t, bg_row, bg_col, conv_qk,
      mh_norm_g)


def _lru_scan(a, u, h0):
    T, C = a.shape
    G = T // SUBLANES
    a = a.reshape(G, SUBLANES, C)
    u = u.reshape(G, SUBLANES, C)
    rid = lax.broadcasted_iota(I32, a.shape, 1)
    k = 1
    while k < SUBLANES:
        keep = rid >= k
        a_sh = jnp.where(keep, pltpu.roll(a, k, axis=1), 1.0)
        u_sh = jnp.where(keep, pltpu.roll(u, k, axis=1), 0.0)
        u = a * u_sh + u
        a = a * a_sh
        k *= 2
    h = h0
    groups = []
    for g in range(G):
        blk = u[g] + a[g] * h
        groups.append(blk)
        h = blk[SUBLANES - 1:SUBLANES, :]
    return jnp.concatenate(groups, axis=0)


def _rglru_kernel(xp_ref, x_ref, gr_ref, cw_ref, cb_ref, wax_ref, ba_ref, bx_ref, lam_ref, ng_ref,
                  y_ref, h_ref):
    t = pl.program_id(1)
    T = x_ref.shape[1]
    RW = x_ref.shape[2]
    bd = RW // R_BLOCKS

    @pl.when(t == 0)
    def _():
        h_ref[...] = jnp.zeros_like(h_ref)

    prev = jnp.where(t > 0, xp_ref[0], 0.0)
    xr = _causal_conv(x_ref[0], prev, cw_ref[...]) + cb_ref[...]
    xrb = xr.astype(BF16)
    ls = _log_sigmoid(lam_ref[...])
    for n in range(R_BLOCKS):
        sl = slice(n * bd, (n + 1) * bd)
        z = jnp.dot(xrb[:, sl], wax_ref[n], preferred_element_type=F32)
        r_gate = jax.nn.sigmoid(z[:, :bd] + ba_ref[:, sl])
        i_gate = jax.nn.sigmoid(z[:, bd:] + bx_ref[:, sl])
        log_a = LRU_C * r_gate * ls[:, sl]
        a = jnp.exp(log_a)
        u = jnp.sqrt(-jnp.tanh(log_a) * (a * a + 1.0)) * (i_gate * xr[:, sl])
        hseq = _lru_scan(a, u, h_ref[:, sl])
        h_ref[:, sl] = hseq[T - 1:T, :]
        y = hseq * jax.nn.gelu(gr_ref[0, :, sl])
        y = y * lax.rsqrt(jnp.mean(y * y, axis=-1, keepdims=True) + EPS) * ng_ref[:, sl]
        y_ref[0, :, sl] = y.astype(BF16)


def _rglru(proj3, lru_conv_w, lru_conv_b, wax, b_a, b_x, lam, ng, T, xr_blk, gr_blk):
    B, S, _ = proj3.shape
    RW = lam.shape[-1]
    bd = RW // R_BLOCKS
    t8 = T // SUBLANES
    vec = pl.BlockSpec((1, RW), lambda b, t: (0, 0))
    return pl.pallas_call(
        _rglru_kernel,
        grid=(B, S // T),
        in_specs=[pl.BlockSpec((1, SUBLANES, RW), lambda b, t: (b, jnp.maximum(t * t8 - 1, 0), xr_blk)),
                  pl.BlockSpec((1, T, RW), lambda b, t: (b, t, xr_blk)),
                  pl.BlockSpec((1, T, RW), lambda b, t: (b, t, gr_blk)),
                  pl.BlockSpec((CONV_WIDTH, RW), lambda b, t: (0, 0)),
                  vec,
                  pl.BlockSpec((R_BLOCKS, bd, 2 * bd), lambda b, t: (0, 0, 0)),
                  vec, vec, vec, vec],
        out_specs=pl.BlockSpec((1, T, RW), lambda b, t: (b, t, 0)),
        out_shape=jax.ShapeDtypeStruct((B, S, RW), BF16),
        scratch_shapes=[pltpu.VMEM((1, RW), F32)],
        compiler_params=_cparams(("arbitrary", "arbitrary")),
        name="rglru",
    )(proj3, proj3, proj3, lru_conv_w, lru_conv_b, wax, b_a, b_x, lam, ng)


def _rows_to_token_tiles(x, stage_ref, tiles_ref):
    T, D = x.shape
    nq = D // LANES
    for q in range(nq):
        stage_ref[pl.ds(q, T, stride=nq), :] = x[:, q * LANES:(q + 1) * LANES]

    def per_token(t, carry):
        tiles_ref[t] = stage_ref[pl.ds(pl.multiple_of(t * nq, nq), nq), :].astype(BF16)
        return carry

    lax.fori_loop(0, T, per_token, 0, unroll=8)


def _token_tiles_to_rows(tiles_ref, stage_ref, dtype):
    T, nq, _ = tiles_ref.shape

    def per_token(t, carry):
        stage_ref[pl.ds(pl.multiple_of(t * nq, nq), nq), :] = tiles_ref[t].astype(F32)
        return carry

    lax.fori_loop(0, T, per_token, 0, unroll=8)
    return jnp.concatenate([stage_ref[pl.ds(q, T, stride=nq), :].astype(dtype) for q in range(nq)],
                           axis=1)


def _outproj_kernel(x_ref, ym_ref, yr_ref, mod_ref, wm_ref, wr_ref, wrt_ref, brt_ref,
                    x1_ref, hp_ref, lt_ref, stage_ref):
    mix = (jnp.dot(ym_ref[...], wm_ref[...], preferred_element_type=F32)
           + jnp.dot(yr_ref[...], wr_ref[...], preferred_element_type=F32))
    x1 = x_ref[...] + mod_ref[0, 2:3, :] * mix
    x1_ref[...] = x1
    ms = jnp.mean(x1 * x1, axis=-1, keepdims=True)
    hn = x1 * lax.rsqrt(ms + EPS) * (1.0 + mod_ref[0, 4:5, :]) + mod_ref[0, 3:4, :]
    lt_ref[...] = lax.dot_general(wrt_ref[...], hn.astype(BF16), _NT,
                                  preferred_element_type=F32) + brt_ref[:, 0:1]
    _rows_to_token_tiles(hn, stage_ref, hp_ref)


def _out_proj(x2, ym2, yr2, mod3, w_out_b, w_rt_t, b_rt, S, tm):
    N, D = x2.shape
    MW = ym2.shape[1]
    RW = yr2.shape[1]
    assert MW == RW, "the two head groups share one row-block size of w_out"
    NR = w_rt_t.shape[0]
    spb = S // tm
    return pl.pallas_call(
        _outproj_kernel,
        grid=(N // tm,),
        in_specs=[pl.BlockSpec((tm, D), lambda i: (i, 0)),
                  pl.BlockSpec((tm, MW), lambda i: (i, 0)),
                  pl.BlockSpec((tm, RW), lambda i: (i, 0)),
                  pl.BlockSpec((1, 6, D), lambda i: (i // spb, 0, 0)),
                  pl.BlockSpec((MW, D), lambda i: (0, 0)),
                  pl.BlockSpec((RW, D), lambda i: (1, 0)),
                  pl.BlockSpec((NR, D), lambda i: (0, 0)),
                  pl.BlockSpec((NR, LANES), lambda i: (0, 0))],
        out_specs=[pl.BlockSpec((tm, D), lambda i: (i, 0)),
                   pl.BlockSpec((tm, D // LANES, LANES), lambda i: (i, 0, 0)),
                   pl.BlockSpec((NR, tm), lambda i: (0, i))],
        out_shape=[jax.ShapeDtypeStruct((N, D), F32),
                   jax.ShapeDtypeStruct((N, D // LANES, LANES), BF16),
                   jax.ShapeDtypeStruct((NR, N), F32)],
        scratch_shapes=[pltpu.VMEM((tm * (D // LANES), LANES), F32)],
        compiler_params=_cparams(("arbitrary",)),
        name="out_proj",
    )(x2, ym2, yr2, mod3, w_out_b, w_out_b, w_rt_t, b_rt)


def _router_kernel(lt_ref, rw_ref, meta_ref, d_ref, ri_scr, pcol_scr, carry_ref):
    phase = pl.program_id(0)
    i = pl.program_id(1)

    @pl.when(phase == 0)
    def _():
        _route_tile(i, lt_ref, ri_scr.at[i], rw_ref, meta_ref, pcol_scr, carry_ref)

    @pl.when(phase == 1)
    def _():
        _slots_tile(ri_scr.at[i], pcol_scr, d_ref)


def _route_tile(i, lt_ref, ri_ref, rw_ref, meta_ref, pcol_ref, carry_ref):
    tm = lt_ref.shape[1]
    E8 = EXPERTS_PER_GROUP

    @pl.when(i == 0)
    def _():
        carry_ref[...] = jnp.zeros_like(carry_ref)
        meta_ref[...] = jnp.zeros_like(meta_ref)
        pcol_ref[...] = jnp.zeros_like(pcol_ref)

    sub = lax.broadcasted_iota(I32, (SUBLANES, tm), 0)
    gl = jnp.where(sub < N_GROUPS, lt_ref[N_EXPERTS:N_EXPERTS + SUBLANES, :], -jnp.inf)
    ge = jnp.exp(gl - jnp.max(gl, axis=0, keepdims=True))
    pg = ge / jnp.sum(ge, axis=0, keepdims=True)
    pg_sel = jnp.max(pg, axis=0, keepdims=True)
    g_sel = jnp.min(jnp.where(pg == pg_sel, sub, SUBLANES), axis=0, keepdims=True)

    el = lt_ref[(N_GROUPS - 1) * E8:N_GROUPS * E8, :]
    for g in range(N_GROUPS - 2, -1, -1):
        el = jnp.where(g_sel == g, lt_ref[g * E8:(g + 1) * E8, :], el)
    ee = jnp.exp(el - jnp.max(el, axis=0, keepdims=True))
    pe = ee / jnp.sum(ee, axis=0, keepdims=True)
    p0 = jnp.max(pe, axis=0, keepdims=True)
    i0 = jnp.min(jnp.where(pe == p0, sub, SUBLANES), axis=0, keepdims=True)
    pe1 = jnp.where(sub == i0, -1.0, pe)
    p1 = jnp.max(pe1, axis=0, keepdims=True)
    i1 = jnp.min(jnp.where(pe1 == p1, sub, SUBLANES), axis=0, keepdims=True)
    psum = p0 + p1
    w0 = pg_sel * p0 / psum
    w1 = pg_sel * p1 / psum
    e0 = g_sel * E8 + i0
    e1 = g_sel * E8 + i1

    eid = lax.broadcasted_iota(I32, (N_EXPERTS, tm), 0)
    oh0 = (eid == e0).astype(F32)
    oh1 = (eid == e1).astype(F32)
    oh = oh0 + oh1
    r_ = lax.broadcasted_iota(I32, (tm, tm), 0)
    c_ = lax.broadcasted_iota(I32, (tm, tm), 1)
    before = (r_ < c_).astype(BF16)
    cnt = carry_ref[:, 0:1] + jnp.dot(oh.astype(BF16), before, preferred_element_type=F32)
    pos0 = jnp.sum(oh0 * cnt, axis=0, keepdims=True)
    pos1 = jnp.sum(oh1 * cnt, axis=0, keepdims=True)
    new_carry = carry_ref[...] + jnp.sum(oh, axis=1, keepdims=True)
    carry_ref[...] = new_carry

    ri = jnp.where(sub == 0, e0, 0)
    ri = jnp.where(sub == 1, e1, ri)
    ri = jnp.where(sub == 2, pos0.astype(I32), ri)
    ri = jnp.where(sub == 3, pos1.astype(I32), ri)
    ri_ref[...] = ri
    rw_ref[...] = jnp.where(sub == 0, w0, jnp.where(sub == 1, w1, 0.0))

    @pl.when(i == pl.num_programs(1) - 1)
    def _():
        nbp = meta_ref.shape[1]
        cnt_col = new_carry[:, 0:1]
        padded_col = jnp.ceil(cnt_col / MOE_ROWS) * MOE_ROWS
        nonempty = cnt_col > 0.0
        e_sub = lax.broadcasted_iota(I32, (N_EXPERTS, nbp), 0)
        lane = lax.broadcasted_iota(I32, (N_EXPERTS, nbp), 1)
        padded_row = jnp.sum(jnp.where(e_sub == lane, padded_col, 0.0), axis=0, keepdims=True)
        pend_row = jnp.sum(jnp.where(e_sub <= lane, padded_col, 0.0), axis=0, keepdims=True)
        pend_col = jnp.sum(jnp.where(lane <= e_sub, padded_row, 0.0), axis=1, keepdims=True)
        blk_start = lane.astype(F32) * MOE_ROWS
        e_f = e_sub.astype(F32)
        be = jnp.sum(jnp.where(pend_col <= blk_start, 1.0, 0.0), axis=0, keepdims=True)
        be = jnp.minimum(be, N_EXPERTS - 1.0)
        nxt = jnp.min(jnp.where(jnp.logical_and(e_f > be, nonempty), e_f, float(N_EXPERTS)),
                      axis=0, keepdims=True)
        run = jnp.sum(jnp.where(jnp.logical_and(e_f < be, nonempty), 1.0, 0.0),
                      axis=0, keepdims=True)
        cnt_row = jnp.sum(jnp.where(e_sub == lane, cnt_col, 0.0), axis=0, keepdims=True)
        n_used = pend_row[:, N_EXPERTS - 1:N_EXPERTS] / MOE_ROWS
        sub8 = lax.broadcasted_iota(I32, (SUBLANES, nbp), 0)
        meta = jnp.where(sub8 == META_BLOCK_E, be, 0.0)
        meta = jnp.where(sub8 == META_NEXT_E, nxt, meta)
        meta = jnp.where(sub8 == META_SLOT, run - 2.0 * jnp.floor(run * 0.5), meta)
        meta = jnp.where(sub8 == META_NUSED, n_used, meta)
        meta = jnp.where(sub8 == META_PAD_LO, pend_row - padded_row + cnt_row, meta)
        meta = jnp.where(sub8 == META_PAD_N, padded_row - cnt_row, meta)
        meta_ref[...] = meta.astype(I32)
        pcol_ref[...] = jnp.broadcast_to(pend_col - padded_col, pcol_ref.shape)


def _router(logits_t, tm, n_blocks):
    NR, N = logits_t.shape
    nt = N // tm
    nbp = max(-(-n_blocks // LANES), 1) * LANES
    routed = lambda p, i: (0, jnp.where(p == 0, i, nt - 1))
    return pl.pallas_call(
        _router_kernel,
        grid=(2, nt),
        in_specs=[pl.BlockSpec((NR, tm), routed)],
        out_specs=[pl.BlockSpec((SUBLANES, tm), routed),
                   pl.BlockSpec((SUBLANES, nbp), lambda p, i: (0, 0)),
                   pl.BlockSpec((SUBLANES, tm), lambda p, i: (0, jnp.where(p == 0, 0, i)))],
        out_shape=[jax.ShapeDtypeStruct((SUBLANES, N), F32),
                   jax.ShapeDtypeStruct((SUBLANES, nbp), I32),
                   jax.ShapeDtypeStruct((SUBLANES, N), I32)],
        scratch_shapes=[pltpu.VMEM((nt, SUBLANES, tm), I32),
                        pltpu.VMEM((N_EXPERTS, LANES), F32),
                        pltpu.VMEM((N_EXPERTS, LANES), F32)],
        compiler_params=_cparams(("arbitrary", "arbitrary")),
        name="router",
    )(logits_t)


def _slots_tile(ri_ref, pcol_ref, d_ref):
    tm = ri_ref.shape[1]
    eid = lax.broadcasted_iota(I32, (N_EXPERTS, tm), 0)
    pstart = pcol_ref[:, 0:1]
    sub = lax.broadcasted_iota(I32, (SUBLANES, tm), 0)
    out = jnp.zeros((SUBLANES, tm), I32)
    for kk in range(TOP_K):
        first = jnp.sum(jnp.where(eid == ri_ref[kk:kk + 1, :], pstart, 0.0), axis=0, keepdims=True)
        out = jnp.where(sub == kk, first.astype(I32) + ri_ref[TOP_K + kk:TOP_K + kk + 1, :], out)
    d_ref[...] = out


ROW_DMA_UNROLL = 8


def _row_copy(src, dst, sem):
    return pltpu.make_async_copy(src, dst, sem)


def _unrolled(n, fn):
    def trip(g, carry):
        for u in range(ROW_DMA_UNROLL):
            fn(g * ROW_DMA_UNROLL + u)
        return carry
    lax.fori_loop(0, n // ROW_DMA_UNROLL, trip, 0)


def _dispatch_kernel(meta_ref, dest_ref, hp_ref, xs_ref, zero_ref, sem):
    i = pl.program_id(0)
    tm = hp_ref.shape[0]
    n_blocks = xs_ref.shape[0] // MOE_ROWS

    def issue(t):
        for kk in range(TOP_K):
            _row_copy(hp_ref.at[t], xs_ref.at[dest_ref[kk, t]], sem.at[0]).start(priority=kk)

    _unrolled(tm, issue)
    _unrolled(TOP_K * tm, lambda t: _row_copy(hp_ref.at[0], xs_ref.at[0], sem.at[0]).wait())

    @pl.when(i == pl.num_programs(0) - 1)
    def _():
        zero_ref[...] = jnp.zeros_like(zero_ref)
        for e in range(N_EXPERTS):
            lo = meta_ref[META_PAD_LO, e]
            n_pad = meta_ref[META_PAD_N, e]

            def fill(r, carry):
                _row_copy(zero_ref.at[0], xs_ref.at[lo + r], sem.at[1]).start()
                return carry

            def drain(r, carry):
                _row_copy(zero_ref.at[0], xs_ref.at[0], sem.at[1]).wait()
                return carry

            lax.fori_loop(0, n_pad, fill, 0)
            lax.fori_loop(0, n_pad, drain, 0)

        def block_copy(b):
            return _row_copy(zero_ref, xs_ref.at[pl.ds(b * MOE_ROWS, MOE_ROWS)], sem.at[1])

        n_used = meta_ref[META_NUSED, 0]
        lax.fori_loop(n_used, n_blocks, lambda b, c: (block_copy(b).start(), c)[1], 0)
        lax.fori_loop(n_used, n_blocks, lambda b, c: (block_copy(0).wait(), c)[1], 0)


def _dispatch(meta, dest, hp, n_slots, tm):
    N, nq, _ = hp.shape
    grid_spec = pltpu.PrefetchScalarGridSpec(
        num_scalar_prefetch=1,
        grid=(N // tm,),
        in_specs=[pl.BlockSpec((SUBLANES, tm), lambda i, m: (0, i), memory_space=pltpu.SMEM),
                  pl.BlockSpec((tm, nq, LANES), lambda i, m: (i, 0, 0))],
        out_specs=pl.BlockSpec(memory_space=pl.ANY),
        scratch_shapes=[pltpu.VMEM((MOE_ROWS, nq, LANES), BF16), pltpu.SemaphoreType.DMA((2,))],
    )
    return pl.pallas_call(
        _dispatch_kernel,
        grid_spec=grid_spec,
        out_shape=jax.ShapeDtypeStruct((n_slots, nq, LANES), BF16),
        compiler_params=_cparams(("arbitrary",)),
        name="dispatch",
    )(meta, dest, hp)


def _expert_kernel(meta_ref, xs_ref, wg_hbm, wu_hbm, wd_hbm, ys_ref, wbuf_g, wbuf_u, wbuf_d,
                   wgu_s, wd_s, stage_ref, sem):
    i = pl.program_id(0)
    DE = wbuf_g.shape[2]
    e = meta_ref[META_BLOCK_E, i]
    e_prev = meta_ref[META_BLOCK_E, jnp.maximum(i - 1, 0)]
    used = i < meta_ref[META_NUSED, 0]
    first = jnp.logical_and(used, jnp.logical_or(i == 0, e != e_prev))
    slot = meta_ref[META_SLOT, i]
    nxt = meta_ref[META_NEXT_E, i]

    def weight_copies(ex, s):
        return (_row_copy(wg_hbm.at[ex], wbuf_g.at[s], sem.at[s]),
                _row_copy(wu_hbm.at[ex], wbuf_u.at[s], sem.at[s]),
                _row_copy(wd_hbm.at[ex], wbuf_d.at[s], sem.at[s]))

    @pl.when(jnp.logical_and(used, i == 0))
    def _():
        for cp in weight_copies(e, slot):
            cp.start()

    @pl.when(first)
    def _():
        @pl.when(nxt < N_EXPERTS)
        def _():
            for cp in weight_copies(nxt, 1 - slot):
                cp.start()

        for cp in weight_copies(e, slot):
            cp.wait()
        wgu_s[:, :DE] = wbuf_g[slot].astype(BF16)
        wgu_s[:, DE:] = wbuf_u[slot].astype(BF16)
        wd_s[...] = wbuf_d[slot].astype(BF16)

    @pl.when(used)
    def _():
        xb = _token_tiles_to_rows(xs_ref, stage_ref, BF16)
        gu = jnp.dot(xb, wgu_s[...], preferred_element_type=F32)
        hb = _silu(gu[:, :DE]) * gu[:, DE:]
        ys_ref[...] = jnp.dot(hb.astype(BF16), wd_s[...], preferred_element_type=F32)

    @pl.when(jnp.logical_not(used))
    def _():
        ys_ref[...] = jnp.zeros_like(ys_ref)


def _experts(meta, xs, w_e_gate, w_e_up, w_e_down):
    n_slots, nq, _ = xs.shape
    _, D, DE = w_e_gate.shape
    n_blocks = n_slots // MOE_ROWS
    hbm = pl.BlockSpec(memory_space=pl.ANY)
    grid_spec = pltpu.PrefetchScalarGridSpec(
        num_scalar_prefetch=1,
        grid=(n_blocks,),
        in_specs=[pl.BlockSpec((MOE_ROWS, nq, LANES), lambda i, m: (i, 0, 0)), hbm, hbm, hbm],
        out_specs=pl.BlockSpec((MOE_ROWS, D), lambda i, m: (i, 0)),
        scratch_shapes=[pltpu.VMEM((2, D, DE), F32), pltpu.VMEM((2, D, DE), F32),
                        pltpu.VMEM((2, DE, D), F32),
                        pltpu.VMEM((D, 2 * DE), BF16), pltpu.VMEM((DE, D), BF16),
                        pltpu.VMEM((MOE_ROWS * nq, LANES), F32),
                        pltpu.SemaphoreType.DMA((2,))],
    )
    return pl.pallas_call(
        _expert_kernel,
        grid_spec=grid_spec,
        out_shape=jax.ShapeDtypeStruct((n_slots, D), F32),
        compiler_params=_cparams(("arbitrary",)),
        name="experts",
    )(meta, xs, w_e_gate, w_e_up, w_e_down)


def _combine_kernel(rcur_ref, rnext_ref, x1_ref, wt_ref, mod_ref, fg_ref, ys_ref, o_ref, gbuf, sem):
    i = pl.program_id(0)
    n = pl.num_programs(0)
    tm = x1_ref.shape[0]
    slot = i % 2

    def gather(dest_ref, s):
        def issue(t):
            for kk in range(TOP_K):
                _row_copy(ys_ref.at[pl.ds(dest_ref[kk, t], 1)], gbuf.at[s, kk, pl.ds(t, 1)],
                          sem.at[s]).start(priority=kk)
        _unrolled(tm, issue)

    @pl.when(i == 0)
    def _():
        gather(rcur_ref, 0)

    @pl.when(i + 1 < n)
    def _():
        gather(rnext_ref, 1 - slot)

    _unrolled(TOP_K * tm,
              lambda t: _row_copy(ys_ref.at[pl.ds(0, 1)], gbuf.at[slot, 0, pl.ds(0, 1)],
                                  sem.at[slot]).wait())

    wt = wt_ref[...]
    y = wt[:, 0:1] * gbuf[slot, 0] + wt[:, 1:2] * gbuf[slot, 1]
    x2 = x1_ref[...] + mod_ref[0, 5:6, :] * y
    ms = jnp.mean(x2 * x2, axis=-1, keepdims=True)
    o_ref[...] = x2 * lax.rsqrt(ms + EPS) * fg_ref[...]


def _combine(dest, x1, wts, mod3, final_g, ys, S, tm):
    N, D = x1.shape
    spb = S // tm
    nsteps = N // tm
    return pl.pallas_call(
        _combine_kernel,
        grid=(nsteps,),
        in_specs=[pl.BlockSpec((SUBLANES, tm), lambda i: (0, i), memory_space=pltpu.SMEM),
                  pl.BlockSpec((SUBLANES, tm), lambda i: (0, jnp.minimum(i + 1, nsteps - 1)),
                               memory_space=pltpu.SMEM),
                  pl.BlockSpec((tm, D), lambda i: (i, 0)),
                  pl.BlockSpec((tm, SUBLANES), lambda i: (i, 0)),
                  pl.BlockSpec((1, 6, D), lambda i: (i // spb, 0, 0)),
                  pl.BlockSpec((1, D), lambda i: (0, 0)),
                  pl.BlockSpec(memory_space=pl.ANY)],
        out_specs=pl.BlockSpec((tm, D), lambda i: (i, 0)),
        out_shape=jax.ShapeDtypeStruct((N, D), F32),
        scratch_shapes=[pltpu.VMEM((2, TOP_K, tm, D), F32), pltpu.SemaphoreType.DMA((2,))],
        compiler_params=_cparams(("arbitrary",)),
        name="combine",
    )(dest, dest, x1, wts, mod3, final_g.reshape(1, D), ys)


def _tiles(S):
    def fit(t):
        return min(t, S)
    return dict(prep_tr=512, inproj_tm=fit(1024), inproj_tn=1536, chunk=fit(256), lru_t=fit(512),
                outproj_tm=fit(512), router_tm=fit(512), dispatch_tm=fit(256), combine_tm=fit(256))


def _layer(x, mod, w_in, b_gates, conv_qk, mh_norm_g, lru_conv_w, lru_conv_b, w_lru_a, b_lru_a,
           w_lru_x, b_lru_x, lru_lambda, lru_norm_g, w_out, w_group, b_group, w_router, b_router,
           w_e_gate, w_e_up, w_e_down, final_g):
    B, S, D = x.shape
    N = B * S
    MW = mh_norm_g.shape[-1]
    RW = lru_lambda.shape[-1]
    H = M_HEADS
    tl = _tiles(S)
    x2 = x.reshape(N, D)
    mod3 = mod.reshape(B, 6, D)

    n_qkvo = 4 * MW
    w_in_t = w_in.T
    w_main_t = _prep_w_in(w_in_t, n_qkvo, tl["prep_tr"])
    w_gate_t = w_in_t[n_qkvo:n_qkvo + 2 * H]
    w_ift = jnp.zeros((SUBLANES, D), F32).at[:2 * H].set(w_gate_t).astype(BF16)
    w_if_t = jnp.zeros((LANES, D), F32).at[:2 * H].set(w_gate_t).astype(BF16)
    proj, gates, gates_t = _in_proj(x2, mod3, w_main_t, w_if_t, w_ift, S, tl["inproj_tm"],
                                    tl["inproj_tn"])
    proj3 = proj.reshape(B, S, -1)

    bg_row = jnp.zeros((1, LANES), F32).at[0, :2 * H].set(b_gates)
    bg_col = jnp.zeros((SUBLANES, LANES), F32).at[:2 * H, :].set(b_gates[:, None])
    ym = _mlstm(proj3, gates.reshape(B, S, LANES), gates_t, bg_row, bg_col, conv_qk,
                mh_norm_g.reshape(1, MW), tl["chunk"])

    wax = jnp.concatenate([w_lru_a, w_lru_x], axis=-1).astype(BF16)
    yr = _rglru(proj3, lru_conv_w, lru_conv_b.reshape(1, RW), wax, b_lru_a.reshape(1, RW),
                b_lru_x.reshape(1, RW), lru_lambda.reshape(1, RW), lru_norm_g.reshape(1, RW),
                tl["lru_t"], n_qkvo // RW, n_qkvo // RW + 1)

    NR = N_EXPERTS + SUBLANES
    w_rt_t = (jnp.zeros((NR, D), F32).at[:N_EXPERTS].set(w_router.T)
              .at[N_EXPERTS:N_EXPERTS + N_GROUPS].set(w_group.T)).astype(BF16)
    b_rt = (jnp.zeros((NR, LANES), F32).at[:N_EXPERTS, :].set(b_router[:, None])
            .at[N_EXPERTS:N_EXPERTS + N_GROUPS, :].set(b_group[:, None]))
    x1, hp, logits_t = _out_proj(x2, ym.reshape(N, MW), yr.reshape(N, RW), mod3,
                                 w_out.astype(BF16), w_rt_t, b_rt, S, tl["outproj_tm"])

    n_slots = N * TOP_K + N_EXPERTS * MOE_ROWS
    rw, meta, dest = _router(logits_t, tl["router_tm"], n_slots // MOE_ROWS)
    xs = _dispatch(meta, dest, hp, n_slots, tl["dispatch_tm"])
    ys = _experts(meta, xs, w_e_gate, w_e_up, w_e_down)
    out = _combine(dest, x1, rw.T, mod3, final_g, ys, S, tl["combine_tm"])
    return out.reshape(B, S, D)


def kernel(x, c, w_ada, b_ada, w_in, b_gates, conv_qk, mh_norm_g, lru_conv_w, lru_conv_b, w_lru_a, b_lru_a, w_lru_x, b_lru_x, lru_lambda, lru_norm_g, w_out, w_group, b_group, w_router, b_router, w_e_gate, w_e_up, w_e_down, final_g):
    depth = w_ada.shape[0]
    assert depth == 1, "single trunk layer"
    l = 0
    mod = _ada_mod(c, w_ada[l], b_ada[l])
    return _layer(x, mod, w_in[l], b_gates[l], conv_qk[l], mh_norm_g[l], lru_conv_w[l],
                  lru_conv_b[l], w_lru_a[l], b_lru_a[l], w_lru_x[l], b_lru_x[l], lru_lambda[l],
                  lru_norm_g[l], w_out[l], w_group[l], b_group[l], w_router[l], b_router[l],
                  w_e_gate[l], w_e_up[l], w_e_down[l], final_g)
```

```python
import functools

import jax
import jax.numpy as jnp
from jax import lax
from jax.experimental import pallas as pl
from jax.experimental.pallas import tpu as pltpu

F32 = jnp.float32
BF16 = jnp.bfloat16
U32 = jnp.uint32
I32 = jnp.int32

EPS = 1e-6
M_HEADS = 4
R_BLOCKS = 8
CONV_WIDTH = 4
LRU_C = 8.0
N_GROUPS = 4
EXPERTS_PER_GROUP = 8
N_EXPERTS = N_GROUPS * EXPERTS_PER_GROUP
TOP_K = 2

LANES = 128
SUBLANES = 8
VMEM_LIMIT = 56 * 1024 * 1024
VMEM_LIMIT_LARGE = 60 * 1024 * 1024
PROLOGUE_ROWS = 256

MOE_ROWS = 256
META_BLOCK_E, META_NEXT_E, META_SLOT, META_NUSED, META_PAD_LO, META_PAD_N = range(6)


def _cparams(sem, vmem=VMEM_LIMIT):
    return pltpu.CompilerParams(dimension_semantics=sem, vmem_limit_bytes=vmem)


def _log_sigmoid(x):
    return jnp.minimum(x, 0.0) - jnp.log1p(jnp.exp(-jnp.abs(x)))


def _silu(x):
    return x * jax.nn.sigmoid(x)


def _split3(x):
    hi = x.astype(BF16)
    r1 = x - hi.astype(F32)
    mid = r1.astype(BF16)
    lo = (r1 - mid.astype(F32)).astype(BF16)
    return hi, mid, lo


def _causal_conv(cur, prev8, w4):
    T, C = cur.shape
    G = T // SUBLANES
    x3 = cur.reshape(G, SUBLANES, C)
    p3 = prev8.reshape(1, SUBLANES, C)
    rid = lax.broadcasted_iota(I32, x3.shape, 1)
    acc = x3 * w4[CONV_WIDTH - 1:CONV_WIDTH, :]
    for d in range(1, CONV_WIDTH):
        rot = pltpu.roll(x3, d, axis=1)
        before = jnp.concatenate([pltpu.roll(p3, d, axis=1), rot[:G - 1]], axis=0)
        acc = acc + jnp.where(rid < d, before, rot) * w4[CONV_WIDTH - 1 - d:CONV_WIDTH - d, :]
    return acc.reshape(T, C)


def _ada_kernel(c_ref, w_ref, b_ref, o_ref):
    s = _silu(c_ref[...])
    o_ref[...] = jnp.dot(s.astype(BF16), w_ref[...].astype(BF16),
                         preferred_element_type=F32) + b_ref[...]


def _ada_mod(c, w_ada, b_ada):
    B, D = c.shape
    n6 = w_ada.shape[1]
    tn = 2048
    cp = jnp.zeros((SUBLANES, D), F32).at[:B].set(c)
    out = pl.pallas_call(
        _ada_kernel,
        grid=(n6 // tn,),
        in_specs=[pl.BlockSpec((SUBLANES, D), lambda j: (0, 0)),
                  pl.BlockSpec((D, tn), lambda j: (0, j)),
                  pl.BlockSpec((1, tn), lambda j: (0, j))],
        out_specs=pl.BlockSpec((SUBLANES, tn), lambda j: (0, j)),
        out_shape=jax.ShapeDtypeStruct((SUBLANES, n6), F32),
        compiler_params=_cparams(("arbitrary",)),
        name="ada_mod",
    )(cp, w_ada, b_ada.reshape(1, n6))
    return out[:B]


_NT = (((1,), (1,)), ((), ()))


def _prep_kernel(w_ref, wm_ref):
    wm_ref[...] = w_ref[...].astype(BF16)


def _prep_w_in(w_in_t, n_qkvo, tr):
    nin, D = w_in_t.shape
    n_gate = 2 * M_HEADS
    nm = nin - n_gate
    assert n_qkvo % tr == 0 and nm % tr == 0 and n_gate % SUBLANES == 0
    return pl.pallas_call(
        _prep_kernel,
        grid=(nm // tr,),
        in_specs=[pl.BlockSpec((pl.Element(tr), pl.Element(D)),
                               lambda i: (pl.multiple_of(
                                   i * tr + jnp.where(i * tr < n_qkvo, 0, n_gate), SUBLANES), 0))],
        out_specs=pl.BlockSpec((tr, D), lambda i: (i, 0)),
        out_shape=jax.ShapeDtypeStruct((nm, D), BF16),
        compiler_params=_cparams(("arbitrary",)),
        name="prep_w_in",
    )(w_in_t)


def _inproj_kernel(x_ref, mod_ref, w_ref, wif_ref, wift_ref, o_ref, g_ref, gt_ref, hn_ref):
    j = pl.program_id(1)

    @pl.when(j == 0)
    def _():
        tm = x_ref.shape[0]
        rc = min(tm, PROLOGUE_ROWS)
        for r in range(tm // rc):
            rows = slice(r * rc, (r + 1) * rc)
            x = x_ref[rows, :]
            ms = jnp.mean(x * x, axis=-1, keepdims=True)
            hn = x * lax.rsqrt(ms + EPS) * (1.0 + mod_ref[0, 1:2, :]) + mod_ref[0, 0:1, :]
            hb = hn.astype(BF16)
            hn_ref[rows, :] = hb
            g_ref[rows, :] = lax.dot_general(hb, wif_ref[...], _NT, preferred_element_type=F32)
            gt_ref[:, rows] = lax.dot_general(wift_ref[...], hb, _NT, preferred_element_type=F32)

    o_ref[...] = lax.dot_general(hn_ref[...], w_ref[...], _NT, preferred_element_type=F32)


def _in_proj(x2, mod3, w_main_t, w_if_t, w_ift, S, tm, tn):
    N, D = x2.shape
    nw = w_main_t.shape[0]
    spb = S // tm
    return pl.pallas_call(
        _inproj_kernel,
        grid=(N // tm, nw // tn),
        in_specs=[pl.BlockSpec((tm, D), lambda i, j: (i, 0)),
                  pl.BlockSpec((1, 6, D), lambda i, j: (i // spb, 0, 0)),
                  pl.BlockSpec((tn, D), lambda i, j: (j, 0)),
                  pl.BlockSpec((LANES, D), lambda i, j: (0, 0)),
                  pl.BlockSpec((SUBLANES, D), lambda i, j: (0, 0))],
        out_specs=[pl.BlockSpec((tm, tn), lambda i, j: (i, j)),
                   pl.BlockSpec((tm, LANES), lambda i, j: (i, 0)),
                   pl.BlockSpec((SUBLANES, tm), lambda i, j: (0, i))],
        out_shape=[jax.ShapeDtypeStruct((N, nw), F32),
                   jax.ShapeDtypeStruct((N, LANES), F32),
                   jax.ShapeDtypeStruct((SUBLANES, N), F32)],
        scratch_shapes=[pltpu.VMEM((tm, D), BF16)],
        compiler_params=_cparams(("arbitrary", "arbitrary"), vmem=VMEM_LIMIT_LARGE),
        name="in_proj",
    )(x2, mod3, w_main_t, w_if_t, w_ift)


def _mlstm_kernel(qp_ref, kp_ref, q_ref, k_ref, v_ref, o_ref, g_ref, gt_ref, bgr_ref, bgc_ref,
                  cw_ref, ng_ref, y_ref, C_ref, n_ref, m_ref):
    c = pl.program_id(1)
    L = q_ref.shape[1]
    MW = q_ref.shape[2]
    dh = MW // M_HEADS
    H = M_HEADS

    @pl.when(c == 0)
    def _():
        C_ref[...] = jnp.zeros_like(C_ref)
        n_ref[...] = jnp.zeros_like(n_ref)
        m_ref[...] = jnp.zeros_like(m_ref)

    row = lax.broadcasted_iota(I32, (L, L), 0)
    col = lax.broadcasted_iota(I32, (L, L), 1)
    causal = col <= row
    tri = causal.astype(BF16)
    tri_t = (row <= col).astype(BF16)

    gb = g_ref[0] + bgr_ref[...]
    gtb = gt_ref[...] + bgc_ref[:, 0:1]
    b_col_all = sum(jnp.dot(tri, p, preferred_element_type=F32) for p in _split3(_log_sigmoid(gb)))
    b_row_all = sum(jnp.dot(p, tri_t, preferred_element_type=F32) for p in _split3(_log_sigmoid(gtb)))

    has_prev = c > 0

    for h in range(H):
        hs = slice(h * dh, (h + 1) * dh)
        ig_col = gb[:, h:h + 1]
        b_col = b_col_all[:, H + h:H + h + 1]
        ig_row = gtb[h:h + 1, :]
        b_row = b_row_all[H + h:H + h + 1, :]
        b_last = b_row[:, L - 1:L]
        m_prev = m_ref[h][0:1, 0:1]

        qprev = jnp.where(has_prev, qp_ref[0, :, hs], 0.0)
        kprev = jnp.where(has_prev, kp_ref[0, :, hs], 0.0)
        q = _silu(_causal_conv(q_ref[0, :, hs], qprev, cw_ref[:, hs]))
        k = _silu(_causal_conv(k_ref[0, :, hs], kprev,
                               cw_ref[:, MW + h * dh:MW + (h + 1) * dh])) * (dh ** -0.5)
        qb = q.astype(BF16)
        kb = k.astype(BF16)
        vb = v_ref[0, :, hs].astype(BF16)

        qk = lax.dot_general(qb, kb, (((1,), (1,)), ((), ())), preferred_element_type=F32)
        dmat = jnp.where(causal, b_col - b_row + ig_row, -jnp.inf)
        inter = b_col + m_prev
        m_t = jnp.maximum(inter, jnp.max(dmat, axis=-1, keepdims=True))
        s = qk * jnp.exp(dmat - m_t)
        e_inter = jnp.exp(inter - m_t)
        C_old = C_ref[h]
        n_old = n_ref[h]
        num = (jnp.dot(s.astype(BF16), vb, preferred_element_type=F32)
               + e_inter * jnp.dot(qb, C_old.astype(BF16), preferred_element_type=F32))
        den = (jnp.sum(s, axis=-1, keepdims=True)
               + e_inter * jnp.sum(q * n_old, axis=-1, keepdims=True))
        hval = num / jnp.maximum(jnp.abs(den), jnp.exp(-m_t))

        g_col = b_last - b_col + ig_col
        g_row = b_last - b_row + ig_row
        m_new = jnp.maximum(b_last + m_prev, jnp.max(g_row, axis=-1, keepdims=True))
        wk = jnp.exp(g_col - m_new)
        decay = jnp.exp(b_last + m_prev - m_new)
        kw = k * wk
        C_ref[h] = decay * C_old + lax.dot_general(kw.astype(BF16), vb, (((0,), (0,)), ((), ())),
                                                   preferred_element_type=F32)
        n_ref[h] = decay * n_old + jnp.sum(kw, axis=0, keepdims=True)
        m_ref[h] = jnp.broadcast_to(m_new, m_ref.shape[1:])

        hnorm = hval * lax.rsqrt(jnp.mean(hval * hval, axis=-1, keepdims=True) + EPS)
        ym = hnorm * ng_ref[:, hs] * jax.nn.sigmoid(o_ref[0, :, hs])
        y_ref[0, :, hs] = ym.astype(BF16)


def _mlstm(proj3, gates3, gates_t, bg_row, bg_col, conv_qk, mh_norm_g, L):
    B, S, _ = proj3.shape
    MW = mh_norm_g.shape[-1]
    dh = MW // M_HEADS
    nc = S // L
    l8 = L // SUBLANES

    def prev_map(colblk):
        return lambda b, c: (b, jnp.maximum(c * l8 - 1, 0), colblk)

    def cur_map(colblk):
        return lambda b, c: (b, c, colblk)

    return pl.pallas_call(
        _mlstm_kernel,
        grid=(B, nc),
        in_specs=[pl.BlockSpec((1, SUBLANES, MW), prev_map(0)),
                  pl.BlockSpec((1, SUBLANES, MW), prev_map(1)),
                  pl.BlockSpec((1, L, MW), cur_map(0)),
                  pl.BlockSpec((1, L, MW), cur_map(1)),
                  pl.BlockSpec((1, L, MW), cur_map(2)),
                  pl.BlockSpec((1, L, MW), cur_map(3)),
                  pl.BlockSpec((1, L, LANES), lambda b, c: (b, c, 0)),
                  pl.BlockSpec((SUBLANES, L), lambda b, c: (0, b * nc + c)),
                  pl.BlockSpec((1, LANES), lambda b, c: (0, 0)),
                  pl.BlockSpec((SUBLANES, LANES), lambda b, c: (0, 0)),
                  pl.BlockSpec((CONV_WIDTH, 2 * MW), lambda b, c: (0, 0)),
                  pl.BlockSpec((1, MW), lambda b, c: (0, 0))],
        out_specs=pl.BlockSpec((1, L, MW), lambda b, c: (b, c, 0)),
        out_shape=jax.ShapeDtypeStruct((B, S, MW), BF16),
        scratch_shapes=[pltpu.VMEM((M_HEADS, dh, dh), F32),
                        pltpu.VMEM((M_HEADS, 1, dh), F32),
                        pltpu.VMEM((M_HEADS, SUBLANES, LANES), F32)],
        compiler_params=_cparams(("arbitrary", "arbitrary")),
        name="mlstm",
    )(proj3, proj3, proj3, proj3, proj3, proj3, gates3, gates_t, bg_row, bg_col, conv_qk,
      mh_norm_g)


def _lru_scan(a, u, h0):
    T, C = a.shape
    G = T // SUBLANES
    a = a.reshape(G, SUBLANES, C)
    u = u.reshape(G, SUBLANES, C)
    rid = lax.broadcasted_iota(I32, a.shape, 1)
    k = 1
    while k < SUBLANES:
        keep = rid >= k
        a_sh = jnp.where(keep, pltpu.roll(a, k, axis=1), 1.0)
        u_sh = jnp.where(keep, pltpu.roll(u, k, axis=1), 0.0)
        u = a * u_sh + u
        a = a * a_sh
        k *= 2
    h = h0
    groups = []
    for g in range(G):
        blk = u[g] + a[g] * h
        groups.append(blk)
        h = blk[SUBLANES - 1:SUBLANES, :]
    return jnp.concatenate(groups, axis=0)


def _rglru_kernel(xp_ref, x_ref, gr_ref, cw_ref, cb_ref, wax_ref, ba_ref, bx_ref, lam_ref, ng_ref,
                  y_ref, h_ref):
    t = pl.program_id(1)
    T = x_ref.shape[1]
    RW = x_ref.shape[2]
    bd = RW // R_BLOCKS

    @pl.when(t == 0)
    def _():
        h_ref[...] = jnp.zeros_like(h_ref)

    prev = jnp.where(t > 0, xp_ref[0], 0.0)
    xr = _causal_conv(x_ref[0], prev, cw_ref[...]) + cb_ref[...]
    xrb = xr.astype(BF16)
    ls = _log_sigmoid(lam_ref[...])
    for n in range(R_BLOCKS):
        sl = slice(n * bd, (n + 1) * bd)
        z = jnp.dot(xrb[:, sl], wax_ref[n], preferred_element_type=F32)
        r_gate = jax.nn.sigmoid(z[:, :bd] + ba_ref[:, sl])
        i_gate = jax.nn.sigmoid(z[:, bd:] + bx_ref[:, sl])
        log_a = LRU_C * r_gate * ls[:, sl]
        a = jnp.exp(log_a)
        u = jnp.sqrt(-jnp.tanh(log_a) * (a * a + 1.0)) * (i_gate * xr[:, sl])
        hseq = _lru_scan(a, u, h_ref[:, sl])
        h_ref[:, sl] = hseq[T - 1:T, :]
        y = hseq * jax.nn.gelu(gr_ref[0, :, sl])
        y = y * lax.rsqrt(jnp.mean(y * y, axis=-1, keepdims=True) + EPS) * ng_ref[:, sl]
        y_ref[0, :, sl] = y.astype(BF16)


def _rglru(proj3, lru_conv_w, lru_conv_b, wax, b_a, b_x, lam, ng, T, xr_blk, gr_blk):
    B, S, _ = proj3.shape
    RW = lam.shape[-1]
    bd = RW // R_BLOCKS
    t8 = T // SUBLANES
    vec = pl.BlockSpec((1, RW), lambda b, t: (0, 0))
    return pl.pallas_call(
        _rglru_kernel,
        grid=(B, S // T),
        in_specs=[pl.BlockSpec((1, SUBLANES, RW), lambda b, t: (b, jnp.maximum(t * t8 - 1, 0), xr_blk)),
                  pl.BlockSpec((1, T, RW), lambda b, t: (b, t, xr_blk)),
                  pl.BlockSpec((1, T, RW), lambda b, t: (b, t, gr_blk)),
                  pl.BlockSpec((CONV_WIDTH, RW), lambda b, t: (0, 0)),
                  vec,
                  pl.BlockSpec((R_BLOCKS, bd, 2 * bd), lambda b, t: (0, 0, 0)),
                  vec, vec, vec, vec],
        out_specs=pl.BlockSpec((1, T, RW), lambda b, t: (b, t, 0)),
        out_shape=jax.ShapeDtypeStruct((B, S, RW), BF16),
        scratch_shapes=[pltpu.VMEM((1, RW), F32)],
        compiler_params=_cparams(("arbitrary", "arbitrary")),
        name="rglru",
    )(proj3, proj3, proj3, lru_conv_w, lru_conv_b, wax, b_a, b_x, lam, ng)


def _rows_to_token_tiles(x, stage_ref, tiles_ref):
    T, D = x.shape
    nq = D // LANES
    for q in range(nq):
        stage_ref[pl.ds(q, T, stride=nq), :] = x[:, q * LANES:(q + 1) * LANES]

    def per_token(t, carry):
        tiles_ref[t] = stage_ref[pl.ds(pl.multiple_of(t * nq, nq), nq), :].astype(BF16)
        return carry

    lax.fori_loop(0, T, per_token, 0, unroll=8)


def _token_tiles_to_rows(tiles_ref, stage_ref, dtype):
    T, nq, _ = tiles_ref.shape

    def per_token(t, carry):
        stage_ref[pl.ds(pl.multiple_of(t * nq, nq), nq), :] = tiles_ref[t].astype(F32)
        return carry

    lax.fori_loop(0, T, per_token, 0, unroll=8)
    return jnp.concatenate([stage_ref[pl.ds(q, T, stride=nq), :].astype(dtype) for q in range(nq)],
                           axis=1)


def _outproj_kernel(x_ref, ym_ref, yr_ref, mod_ref, wm_ref, wr_ref, wrt_ref, brt_ref,
                    x1_ref, hp_ref, lt_ref, stage_ref):
    mix = (jnp.dot(ym_ref[...], wm_ref[...], preferred_element_type=F32)
           + jnp.dot(yr_ref[...], wr_ref[...], preferred_element_type=F32))
    x1 = x_ref[...] + mod_ref[0, 2:3, :] * mix
    x1_ref[...] = x1
    ms = jnp.mean(x1 * x1, axis=-1, keepdims=True)
    hn = x1 * lax.rsqrt(ms + EPS) * (1.0 + mod_ref[0, 4:5, :]) + mod_ref[0, 3:4, :]
    lt_ref[...] = lax.dot_general(wrt_ref[...], hn.astype(BF16), _NT,
                                  preferred_element_type=F32) + brt_ref[:, 0:1]
    _rows_to_token_tiles(hn, stage_ref, hp_ref)


def _out_proj(x2, ym2, yr2, mod3, w_out_b, w_rt_t, b_rt, S, tm):
    N, D = x2.shape
    MW = ym2.shape[1]
    RW = yr2.shape[1]
    assert MW == RW, "the two head groups share one row-block size of w_out"
    NR = w_rt_t.shape[0]
    spb = S // tm
    return pl.pallas_call(
        _outproj_kernel,
        grid=(N // tm,),
        in_specs=[pl.BlockSpec((tm, D), lambda i: (i, 0)),
                  pl.BlockSpec((tm, MW), lambda i: (i, 0)),
                  pl.BlockSpec((tm, RW), lambda i: (i, 0)),
                  pl.BlockSpec((1, 6, D), lambda i: (i // spb, 0, 0)),
                  pl.BlockSpec((MW, D), lambda i: (0, 0)),
                  pl.BlockSpec((RW, D), lambda i: (1, 0)),
                  pl.BlockSpec((NR, D), lambda i: (0, 0)),
                  pl.BlockSpec((NR, LANES), lambda i: (0, 0))],
        out_specs=[pl.BlockSpec((tm, D), lambda i: (i, 0)),
                   pl.BlockSpec((tm, D // LANES, LANES), lambda i: (i, 0, 0)),
                   pl.BlockSpec((NR, tm), lambda i: (0, i))],
        out_shape=[jax.ShapeDtypeStruct((N, D), F32),
                   jax.ShapeDtypeStruct((N, D // LANES, LANES), BF16),
                   jax.ShapeDtypeStruct((NR, N), F32)],
        scratch_shapes=[pltpu.VMEM((tm * (D // LANES), LANES), F32)],
        compiler_params=_cparams(("arbitrary",)),
        name="out_proj",
    )(x2, ym2, yr2, mod3, w_out_b, w_out_b, w_rt_t, b_rt)


def _router_kernel(lt_ref, rw_ref, meta_ref, d_ref, ri_scr, pcol_scr, carry_ref):
    phase = pl.program_id(0)
    i = pl.program_id(1)

    @pl.when(phase == 0)
    def _():
        _route_tile(i, lt_ref, ri_scr.at[i], rw_ref, meta_ref, pcol_scr, carry_ref)

    @pl.when(phase == 1)
    def _():
        _slots_tile(ri_scr.at[i], pcol_scr, d_ref)


def _route_tile(i, lt_ref, ri_ref, rw_ref, meta_ref, pcol_ref, carry_ref):
    tm = lt_ref.shape[1]
    E8 = EXPERTS_PER_GROUP

    @pl.when(i == 0)
    def _():
        carry_ref[...] = jnp.zeros_like(carry_ref)
        meta_ref[...] = jnp.zeros_like(meta_ref)
        pcol_ref[...] = jnp.zeros_like(pcol_ref)

    sub = lax.broadcasted_iota(I32, (SUBLANES, tm), 0)
    gl = jnp.where(sub < N_GROUPS, lt_ref[N_EXPERTS:N_EXPERTS + SUBLANES, :], -jnp.inf)
    ge = jnp.exp(gl - jnp.max(gl, axis=0, keepdims=True))
    pg = ge / jnp.sum(ge, axis=0, keepdims=True)
    pg_sel = jnp.max(pg, axis=0, keepdims=True)
    g_sel = jnp.min(jnp.where(pg == pg_sel, sub, SUBLANES), axis=0, keepdims=True)

    el = lt_ref[(N_GROUPS - 1) * E8:N_GROUPS * E8, :]
    for g in range(N_GROUPS - 2, -1, -1):
        el = jnp.where(g_sel == g, lt_ref[g * E8:(g + 1) * E8, :], el)
    ee = jnp.exp(el - jnp.max(el, axis=0, keepdims=True))
    pe = ee / jnp.sum(ee, axis=0, keepdims=True)
    p0 = jnp.max(pe, axis=0, keepdims=True)
    i0 = jnp.min(jnp.where(pe == p0, sub, SUBLANES), axis=0, keepdims=True)
    pe1 = jnp.where(sub == i0, -1.0, pe)
    p1 = jnp.max(pe1, axis=0, keepdims=True)
    i1 = jnp.min(jnp.where(pe1 == p1, sub, SUBLANES), axis=0, keepdims=True)
    psum = p0 + p1
    w0 = pg_sel * p0 / psum
    w1 = pg_sel * p1 / psum
    e0 = g_sel * E8 + i0
    e1 = g_sel * E8 + i1

    eid = lax.broadcasted_iota(I32, (N_EXPERTS, tm), 0)
    oh0 = (eid == e0).astype(F32)
    oh1 = (eid == e1).astype(F32)
    oh = oh0 + oh1
    r_ = lax.broadcasted_iota(I32, (tm, tm), 0)
    c_ = lax.broadcasted_iota(I32, (tm, tm), 1)
    before = (r_ < c_).astype(BF16)
    cnt = carry_ref[:, 0:1] + jnp.dot(oh.astype(BF16), before, preferred_element_type=F32)
    pos0 = jnp.sum(oh0 * cnt, axis=0, keepdims=True)
    pos1 = jnp.sum(oh1 * cnt, axis=0, keepdims=True)
    new_carry = carry_ref[...] + jnp.sum(oh, axis=1, keepdims=True)
    carry_ref[...] = new_carry

    ri = jnp.where(sub == 0, e0, 0)
    ri = jnp.where(sub == 1, e1, ri)
    ri = jnp.where(sub == 2, pos0.astype(I32), ri)
    ri = jnp.where(sub == 3, pos1.astype(I32), ri)
    ri_ref[...] = ri
    rw_ref[...] = jnp.where(sub == 0, w0, jnp.where(sub == 1, w1, 0.0))

    @pl.when(i == pl.num_programs(1) - 1)
    def _():
        nbp = meta_ref.shape[1]
        cnt_col = new_carry[:, 0:1]
        padded_col = jnp.ceil(cnt_col / MOE_ROWS) * MOE_ROWS
        nonempty = cnt_col > 0.0
        e_sub = lax.broadcasted_iota(I32, (N_EXPERTS, nbp), 0)
        lane = lax.broadcasted_iota(I32, (N_EXPERTS, nbp), 1)
        padded_row = jnp.sum(jnp.where(e_sub == lane, padded_col, 0.0), axis=0, keepdims=True)
        pend_row = jnp.sum(jnp.where(e_sub <= lane, padded_col, 0.0), axis=0, keepdims=True)
        pend_col = jnp.sum(jnp.where(lane <= e_sub, padded_row, 0.0), axis=1, keepdims=True)
        blk_start = lane.astype(F32) * MOE_ROWS
        e_f = e_sub.astype(F32)
        be = jnp.sum(jnp.where(pend_col <= blk_start, 1.0, 0.0), axis=0, keepdims=True)
        be = jnp.minimum(be, N_EXPERTS - 1.0)
        nxt = jnp.min(jnp.where(jnp.logical_and(e_f > be, nonempty), e_f, float(N_EXPERTS)),
                      axis=0, keepdims=True)
        run = jnp.sum(jnp.where(jnp.logical_and(e_f < be, nonempty), 1.0, 0.0),
                      axis=0, keepdims=True)
        cnt_row = jnp.sum(jnp.where(e_sub == lane, cnt_col, 0.0), axis=0, keepdims=True)
        n_used = pend_row[:, N_EXPERTS - 1:N_EXPERTS] / MOE_ROWS
        sub8 = lax.broadcasted_iota(I32, (SUBLANES, nbp), 0)
        meta = jnp.where(sub8 == META_BLOCK_E, be, 0.0)
        meta = jnp.where(sub8 == META_NEXT_E, nxt, meta)
        meta = jnp.where(sub8 == META_SLOT, run - 2.0 * jnp.floor(run * 0.5), meta)
        meta = jnp.where(sub8 == META_NUSED, n_used, meta)
        meta = jnp.where(sub8 == META_PAD_LO, pend_row - padded_row + cnt_row, meta)
        meta = jnp.where(sub8 == META_PAD_N, padded_row - cnt_row, meta)
        meta_ref[...] = meta.astype(I32)
        pcol_ref[...] = jnp.broadcast_to(pend_col - padded_col, pcol_ref.shape)


def _router(logits_t, tm, n_blocks):
    NR, N = logits_t.shape
    nt = N // tm
    nbp = max(-(-n_blocks // LANES), 1) * LANES
    routed = lambda p, i: (0, jnp.where(p == 0, i, nt - 1))
    return pl.pallas_call(
        _router_kernel,
        grid=(2, nt),
        in_specs=[pl.BlockSpec((NR, tm), routed)],
        out_specs=[pl.BlockSpec((SUBLANES, tm), routed),
                   pl.BlockSpec((SUBLANES, nbp), lambda p, i: (0, 0)),
                   pl.BlockSpec((SUBLANES, tm), lambda p, i: (0, jnp.where(p == 0, 0, i)))],
        out_shape=[jax.ShapeDtypeStruct((SUBLANES, N), F32),
                   jax.ShapeDtypeStruct((SUBLANES, nbp), I32),
                   jax.ShapeDtypeStruct((SUBLANES, N), I32)],
        scratch_shapes=[pltpu.VMEM((nt, SUBLANES, tm), I32),
                        pltpu.VMEM((N_EXPERTS, LANES), F32),
                        pltpu.VMEM((N_EXPERTS, LANES), F32)],
        compiler_params=_cparams(("arbitrary", "arbitrary")),
        name="router",
    )(logits_t)


def _slots_tile(ri_ref, pcol_ref, d_ref):
    tm = ri_ref.shape[1]
    eid = lax.broadcasted_iota(I32, (N_EXPERTS, tm), 0)
    pstart = pcol_ref[:, 0:1]
    sub = lax.broadcasted_iota(I32, (SUBLANES, tm), 0)
    out = jnp.zeros((SUBLANES, tm), I32)
    for kk in range(TOP_K):
        first = jnp.sum(jnp.where(eid == ri_ref[kk:kk + 1, :], pstart, 0.0), axis=0, keepdims=True)
        out = jnp.where(sub == kk, first.astype(I32) + ri_ref[TOP_K + kk:TOP_K + kk + 1, :], out)
    d_ref[...] = out


ROW_DMA_UNROLL = 8


def _row_copy(src, dst, sem):
    return pltpu.make_async_copy(src, dst, sem)


def _unrolled(n, fn):
    def trip(g, carry):
        for u in range(ROW_DMA_UNROLL):
            fn(g * ROW_DMA_UNROLL + u)
        return carry
    lax.fori_loop(0, n // ROW_DMA_UNROLL, trip, 0)


def _dispatch_kernel(meta_ref, dest_ref, hp_ref, xs_ref, zero_ref, sem):
    i = pl.program_id(0)
    tm = hp_ref.shape[0]
    n_blocks = xs_ref.shape[0] // MOE_ROWS

    def issue(t):
        for kk in range(TOP_K):
            _row_copy(hp_ref.at[t], xs_ref.at[dest_ref[kk, t]], sem.at[0]).start(priority=kk)

    _unrolled(tm, issue)
    _unrolled(TOP_K * tm, lambda t: _row_copy(hp_ref.at[0], xs_ref.at[0], sem.at[0]).wait())

    @pl.when(i == pl.num_programs(0) - 1)
    def _():
        zero_ref[...] = jnp.zeros_like(zero_ref)
        for e in range(N_EXPERTS):
            lo = meta_ref[META_PAD_LO, e]
            n_pad = meta_ref[META_PAD_N, e]

            def fill(r, carry):
                _row_copy(zero_ref.at[0], xs_ref.at[lo + r], sem.at[1]).start()
                return carry

            def drain(r, carry):
                _row_copy(zero_ref.at[0], xs_ref.at[0], sem.at[1]).wait()
                return carry

            lax.fori_loop(0, n_pad, fill, 0)
            lax.fori_loop(0, n_pad, drain, 0)

        def block_copy(b):
            return _row_copy(zero_ref, xs_ref.at[pl.ds(b * MOE_ROWS, MOE_ROWS)], sem.at[1])

        n_used = meta_ref[META_NUSED, 0]
        lax.fori_loop(n_used, n_blocks, lambda b, c: (block_copy(b).start(), c)[1], 0)
        lax.fori_loop(n_used, n_blocks, lambda b, c: (block_copy(0).wait(), c)[1], 0)


def _dispatch(meta, dest, hp, n_slots, tm):
    N, nq, _ = hp.shape
    grid_spec = pltpu.PrefetchScalarGridSpec(
        num_scalar_prefetch=1,
        grid=(N // tm,),
        in_specs=[pl.BlockSpec((SUBLANES, tm), lambda i, m: (0, i), memory_space=pltpu.SMEM),
                  pl.BlockSpec((tm, nq, LANES), lambda i, m: (i, 0, 0))],
        out_specs=pl.BlockSpec(memory_space=pl.ANY),
        scratch_shapes=[pltpu.VMEM((MOE_ROWS, nq, LANES), BF16), pltpu.SemaphoreType.DMA((2,))],
    )
    return pl.pallas_call(
        _dispatch_kernel,
        grid_spec=grid_spec,
        out_shape=jax.ShapeDtypeStruct((n_slots, nq, LANES), BF16),
        compiler_params=_cparams(("arbitrary",)),
        name="dispatch",
    )(meta, dest, hp)


def _expert_kernel(meta_ref, xs_ref, wg_hbm, wu_hbm, wd_hbm, ys_ref, wbuf_g, wbuf_u, wbuf_d,
                   wgu_s, wd_s, stage_ref, sem):
    i = pl.program_id(0)
    DE = wbuf_g.shape[2]
    e = meta_ref[META_BLOCK_E, i]
    e_prev = meta_ref[META_BLOCK_E, jnp.maximum(i - 1, 0)]
    used = i < meta_ref[META_NUSED, 0]
    first = jnp.logical_and(used, jnp.logical_or(i == 0, e != e_prev))
    slot = meta_ref[META_SLOT, i]
    nxt = meta_ref[META_NEXT_E, i]

    def weight_copies(ex, s):
        return (_row_copy(wg_hbm.at[ex], wbuf_g.at[s], sem.at[s]),
                _row_copy(wu_hbm.at[ex], wbuf_u.at[s], sem.at[s]),
                _row_copy(wd_hbm.at[ex], wbuf_d.at[s], sem.at[s]))

    @pl.when(jnp.logical_and(used, i == 0))
    def _():
        for cp in weight_copies(e, slot):
            cp.start()

    @pl.when(first)
    def _():
        @pl.when(nxt < N_EXPERTS)
        def _():
            for cp in weight_copies(nxt, 1 - slot):
                cp.start()

        for cp in weight_copies(e, slot):
            cp.wait()
        wgu_s[:, :DE] = wbuf_g[slot].astype(BF16)
        wgu_s[:, DE:] = wbuf_u[slot].astype(BF16)
        wd_s[...] = wbuf_d[slot].astype(BF16)

    @pl.when(used)
    def _():
        xb = _token_tiles_to_rows(xs_ref, stage_ref, BF16)
        gu = jnp.dot(xb, wgu_s[...], preferred_element_type=F32)
        hb = _silu(gu[:, :DE]) * gu[:, DE:]
        ys_ref[...] = jnp.dot(hb.astype(BF16), wd_s[...], preferred_element_type=F32)

    @pl.when(jnp.logical_not(used))
    def _():
        ys_ref[...] = jnp.zeros_like(ys_ref)


def _experts(meta, xs, w_e_gate, w_e_up, w_e_down):
    n_slots, nq, _ = xs.shape
    _, D, DE = w_e_gate.shape
    n_blocks = n_slots // MOE_ROWS
    hbm = pl.BlockSpec(memory_space=pl.ANY)
    grid_spec = pltpu.PrefetchScalarGridSpec(
        num_scalar_prefetch=1,
        grid=(n_blocks,),
        in_specs=[pl.BlockSpec((MOE_ROWS, nq, LANES), lambda i, m: (i, 0, 0)), hbm, hbm, hbm],
        out_specs=pl.BlockSpec((MOE_ROWS, D), lambda i, m: (i, 0)),
        scratch_shapes=[pltpu.VMEM((2, D, DE), F32), pltpu.VMEM((2, D, DE), F32),
                        pltpu.VMEM((2, DE, D), F32),
                        pltpu.VMEM((D, 2 * DE), BF16), pltpu.VMEM((DE, D), BF16),
                        pltpu.VMEM((MOE_ROWS * nq, LANES), F32),
                        pltpu.SemaphoreType.DMA((2,))],
    )
    return pl.pallas_call(
        _expert_kernel,
        grid_spec=grid_spec,
        out_shape=jax.ShapeDtypeStruct((n_slots, D), F32),
        compiler_params=_cparams(("arbitrary",)),
        name="experts",
    )(meta, xs, w_e_gate, w_e_up, w_e_down)


def _combine_kernel(rcur_ref, rnext_ref, x1_ref, wt_ref, mod_ref, fg_ref, ys_ref, o_ref, gbuf, sem):
    i = pl.program_id(0)
    n = pl.num_programs(0)
    tm = x1_ref.shape[0]
    slot = i % 2

    def gather(dest_ref, s):
        def issue(t):
            for kk in range(TOP_K):
                _row_copy(ys_ref.at[pl.ds(dest_ref[kk, t], 1)], gbuf.at[s, kk, pl.ds(t, 1)],
                          sem.at[s]).start(priority=kk)
        _unrolled(tm, issue)

    @pl.when(i == 0)
    def _():
        gather(rcur_ref, 0)

    @pl.when(i + 1 < n)
    def _():
        gather(rnext_ref, 1 - slot)

    _unrolled(TOP_K * tm,
              lambda t: _row_copy(ys_ref.at[pl.ds(0, 1)], gbuf.at[slot, 0, pl.ds(0, 1)],
                                  sem.at[slot]).wait())

    wt = wt_ref[...]
    y = wt[:, 0:1] * gbuf[slot, 0] + wt[:, 1:2] * gbuf[slot, 1]
    x2 = x1_ref[...] + mod_ref[0, 5:6, :] * y
    ms = jnp.mean(x2 * x2, axis=-1, keepdims=True)
    o_ref[...] = x2 * lax.rsqrt(ms + EPS) * fg_ref[...]


def _combine(dest, x1, wts, mod3, final_g, ys, S, tm):
    N, D = x1.shape
    spb = S // tm
    nsteps = N // tm
    return pl.pallas_call(
        _combine_kernel,
        grid=(nsteps,),
        in_specs=[pl.BlockSpec((SUBLANES, tm), lambda i: (0, i), memory_space=pltpu.SMEM),
                  pl.BlockSpec((SUBLANES, tm), lambda i: (0, jnp.minimum(i + 1, nsteps - 1)),
                               memory_space=pltpu.SMEM),
                  pl.BlockSpec((tm, D), lambda i: (i, 0)),
                  pl.BlockSpec((tm, SUBLANES), lambda i: (i, 0)),
                  pl.BlockSpec((1, 6, D), lambda i: (i // spb, 0, 0)),
                  pl.BlockSpec((1, D), lambda i: (0, 0)),
                  pl.BlockSpec(memory_space=pl.ANY)],
        out_specs=pl.BlockSpec((tm, D), lambda i: (i, 0)),
        out_shape=jax.ShapeDtypeStruct((N, D), F32),
        scratch_shapes=[pltpu.VMEM((2, TOP_K, tm, D), F32), pltpu.SemaphoreType.DMA((2,))],
        compiler_params=_cparams(("arbitrary",)),
        name="combine",
    )(dest, dest, x1, wts, mod3, final_g.reshape(1, D), ys)


def _tiles(S):
    def fit(t):
        return min(t, S)
    return dict(prep_tr=512, inproj_tm=fit(1024), inproj_tn=2048, chunk=fit(256), lru_t=fit(512),
                outproj_tm=fit(512), router_tm=fit(512), dispatch_tm=fit(256), combine_tm=fit(256))


def _layer(x, mod, w_in, b_gates, conv_qk, mh_norm_g, lru_conv_w, lru_conv_b, w_lru_a, b_lru_a,
           w_lru_x, b_lru_x, lru_lambda, lru_norm_g, w_out, w_group, b_group, w_router, b_router,
           w_e_gate, w_e_up, w_e_down, final_g):
    B, S, D = x.shape
    N = B * S
    MW = mh_norm_g.shape[-1]
    RW = lru_lambda.shape[-1]
    H = M_HEADS
    tl = _tiles(S)
    x2 = x.reshape(N, D)
    mod3 = mod.reshape(B, 6, D)

    n_qkvo = 4 * MW
    w_in_t = w_in.T
    w_main_t = _prep_w_in(w_in_t, n_qkvo, tl["prep_tr"])
    w_gate_t = w_in_t[n_qkvo:n_qkvo + 2 * H]
    w_ift = jnp.zeros((SUBLANES, D), F32).at[:2 * H].set(w_gate_t).astype(BF16)
    w_if_t = jnp.zeros((LANES, D), F32).at[:2 * H].set(w_gate_t).astype(BF16)
    proj, gates, gates_t = _in_proj(x2, mod3, w_main_t, w_if_t, w_ift, S, tl["inproj_tm"],
                                    tl["inproj_tn"])
    proj3 = proj.reshape(B, S, -1)

    bg_row = jnp.zeros((1, LANES), F32).at[0, :2 * H].set(b_gates)
    bg_col = jnp.zeros((SUBLANES, LANES), F32).at[:2 * H, :].set(b_gates[:, None])
    ym = _mlstm(proj3, gates.reshape(B, S, LANES), gates_t, bg_row, bg_col, conv_qk,
                mh_norm_g.reshape(1, MW), tl["chunk"])

    wax = jnp.concatenate([w_lru_a, w_lru_x], axis=-1).astype(BF16)
    yr = _rglru(proj3, lru_conv_w, lru_conv_b.reshape(1, RW), wax, b_lru_a.reshape(1, RW),
                b_lru_x.reshape(1, RW), lru_lambda.reshape(1, RW), lru_norm_g.reshape(1, RW),
                tl["lru_t"], n_qkvo // RW, n_qkvo // RW + 1)

    NR = N_EXPERTS + SUBLANES
    w_rt_t = (jnp.zeros((NR, D), F32).at[:N_EXPERTS].set(w_router.T)
              .at[N_EXPERTS:N_EXPERTS + N_GROUPS].set(w_group.T)).astype(BF16)
    b_rt = (jnp.zeros((NR, LANES), F32).at[:N_EXPERTS, :].set(b_router[:, None])
            .at[N_EXPERTS:N_EXPERTS + N_GROUPS, :].set(b_group[:, None]))
    x1, hp, logits_t = _out_proj(x2, ym.reshape(N, MW), yr.reshape(N, RW), mod3,
                                 w_out.astype(BF16), w_rt_t, b_rt, S, tl["outproj_tm"])

    n_slots = N * TOP_K + N_EXPERTS * MOE_ROWS
    rw, meta, dest = _router(logits_t, tl["router_tm"], n_slots // MOE_ROWS)
    xs = _dispatch(meta, dest, hp, n_slots, tl["dispatch_tm"])
    ys = _experts(meta, xs, w_e_gate, w_e_up, w_e_down)
    out = _combine(dest, x1, rw.T, mod3, final_g, ys, S, tl["combine_tm"])
    return out.reshape(B, S, D)


def kernel(x, c, w_ada, b_ada, w_in, b_gates, conv_qk, mh_norm_g, lru_conv_w, lru_conv_b, w_lru_a, b_lru_a, w_lru_x, b_lru_x, lru_lambda, lru_norm_g, w_out, w_group, b_group, w_router, b_router, w_e_gate, w_e_up, w_e_down, final_g):
    depth = w_ada.shape[0]
    assert depth == 1, "single trunk layer"
    l = 0
    mod = _ada_mod(c, w_ada[l], b_ada[l])
    return _layer(x, mod, w_in[l], b_gates[l], conv_qk[l], mh_norm_g[l], lru_conv_w[l],
                  lru_conv_b[l], w_lru_a[l], b_lru_a[l], w_lru_x[l], b_lru_x[l], lru_lambda[l],
                  lru_norm_g[l], w_out[l], w_group[l], b_group[l], w_router[l], b_router[l],
                  w_e_gate[l], w_e_up[l], w_e_down[l], final_g)
```
